```python
import math
import jax, jax.numpy as jnp
from jax import lax
import numpy as np

D_MODEL = 1024
BATCH = 32
SEQ = 2048
DEPTH = 4
DEC_BATCH = 16
DEC_SEQ = 2048
PAST_LEN = 128

MIX_HALF = D_MODEL // 2
SHORT_K = 3
CONF_K = 31
D_FF = 4 * D_MODEL
FILTER_ORDER = 64
FILTER_BANDS = 16
FILTER_EMB = 1 + 2 * FILTER_BANDS
DECAY_TARGET = 1e-2
FAST_DECAY_PCT = 0.3
SLOW_DECAY_PCT = 1.5
N_EVEN = (DEPTH + 1) // 2
N_ODD = DEPTH // 2
ALPHA = (2 * DEPTH) ** 0.25
BETA = (8 * DEPTH) ** -0.25
LN_EPS = 1e-5

kernel_name = 'hybrid_shortconv_hyena_conformer_encoder'


def layer_norm(x, g, b):
    xf = x.astype(jnp.float32)
    mu = jnp.mean(xf, axis=-1, keepdims=True)
    xc = xf - mu
    var = jnp.mean(xc * xc, axis=-1, keepdims=True)
    return (xc * lax.rsqrt(var + LN_EPS) * g.astype(jnp.float32) + b.astype(jnp.float32)).astype(x.dtype)


def dwconv(x, w):
    K, C = w.shape
    return lax.conv_general_dilated(
        x, w.astype(x.dtype)[:, None, :], window_strides=(1,), padding=[(K // 2, K // 2)],
        dimension_numbers=('NWC', 'WIO', 'NWC'), feature_group_count=C)


def hyena_filter(L, w1, b1, w2, b2, w3, b3, w4, freq):
    f32 = jnp.float32
    t = jnp.linspace(0.0, 1.0, L, dtype=f32)[:, None]
    w = (2.0 * math.pi / L) * jnp.arange(L, dtype=f32)[:, None]
    bands = jnp.linspace(1e-4, FILTER_BANDS - 1, FILTER_BANDS, dtype=f32)[None, :]
    z = jnp.concatenate([t, jnp.cos(bands * w), -jnp.sin(bands * w)], axis=-1)
    fr = freq.astype(f32)
    h = jnp.sin(fr * (z @ w1.astype(f32) + b1.astype(f32)))
    h = jnp.sin(fr * (h @ w2.astype(f32) + b2.astype(f32)))
    h = jnp.sin(fr * (h @ w3.astype(f32) + b3.astype(f32)))
    h = h @ w4.astype(f32)
    max_decay = math.log(DECAY_TARGET) / FAST_DECAY_PCT
    min_decay = math.log(DECAY_TARGET) / SLOW_DECAY_PCT
    deltas = jnp.abs(jnp.linspace(min_decay, max_decay, MIX_HALF, dtype=f32))
    decay = jnp.exp(-t * deltas[None, :])
    h_fwd = h[:, :MIX_HALF] * decay
    h_bwd = h[:, MIX_HALF:] * decay
    k = jnp.concatenate([h_fwd, jnp.zeros((1, MIX_HALF), f32), h_bwd[:0:-1]], axis=0)
    return k / jnp.sum(jnp.abs(k), axis=0, keepdims=True)


def long_conv(u, k):
    L = u.shape[1]
    U = jnp.fft.rfft(u.astype(jnp.float32), n=2 * L, axis=1)
    Kf = jnp.fft.rfft(k, axis=0)
    return jnp.fft.irfft(U * Kf[None], n=2 * L, axis=1)[:, :L]


def even_mixer(x, w_in, conv_a, short_w, short_b, fw1, fb1, fw2, fb2, fw3, fb3, fw4, freq, hy_bias, w_out):
    H = MIX_HALF
    p = x @ w_in
    a_b, a_c, a_h, hy = p[..., :H], p[..., H:2 * H], p[..., 2 * H:3 * H], p[..., 3 * H:]
    y_a = a_b * dwconv(a_c * a_h, conv_a)
    hy = dwconv(hy, short_w) + short_b
    x0, x1, v = jnp.split(hy, 3, axis=-1)
    u = x1 * v
    k = hyena_filter(x.shape[1], fw1, fb1, fw2, fb2, fw3, fb3, fw4, freq)
    y_b = x0 * (long_conv(u, k).astype(u.dtype) + hy_bias * u)
    return jnp.concatenate([y_a, y_b], axis=-1) @ w_out


def conformer_conv(x, w_pw1, b_pw1, dw_w, dw_b, ln_g, ln_b, w_pw2, b_pw2):
    h = x @ w_pw1 + b_pw1
    a, g = jnp.split(h, 2, axis=-1)
    h = a * jax.nn.sigmoid(g)
    h = dwconv(h, dw_w) + dw_b
    h = jax.nn.silu(layer_norm(h, ln_g, ln_b))
    return h @ w_pw2 + b_pw2


def sq_relu_mlp(x, w1, w2):
    return jnp.square(jax.nn.relu(x @ w1)) @ w2


def setup_inputs(seed: int = 0) -> dict:
    key = jax.random.key(seed)
    ks = iter(jax.random.split(key, 48))
    D, H = D_MODEL, MIX_HALF

    def nrm(shape, scale):
        return jax.random.normal(next(ks), shape, jnp.float32) * scale

    return {
        'x_prompt': nrm((BATCH, SEQ, D), 1.0),
        'x_sample': nrm((DEC_BATCH, DEC_SEQ, D), 1.0),
        'e_w_in': nrm((N_EVEN, D, 6 * H), D ** -0.5),
        'e_conv_a': nrm((N_EVEN, SHORT_K, H), SHORT_K ** -0.5),
        'e_short_w': nrm((N_EVEN, SHORT_K, 3 * H), SHORT_K ** -0.5),
        'e_short_b': nrm((N_EVEN, 3 * H), 0.02),
        'e_flt_w1': nrm((N_EVEN, FILTER_EMB, FILTER_ORDER), FILTER_EMB ** -0.5),
        'e_flt_b1': nrm((N_EVEN, FILTER_ORDER), 0.02),
        'e_flt_w2': nrm((N_EVEN, FILTER_ORDER, FILTER_ORDER), FILTER_ORDER ** -0.5),
        'e_flt_b2': nrm((N_EVEN, FILTER_ORDER), 0.02),
        'e_flt_w3': nrm((N_EVEN, FILTER_ORDER, FILTER_ORDER), FILTER_ORDER ** -0.5),
        'e_flt_b3': nrm((N_EVEN, FILTER_ORDER), 0.02),
        'e_flt_w4': nrm((N_EVEN, FILTER_ORDER, 2 * H), FILTER_ORDER ** -0.5),
        'e_flt_freq': 1.0 + nrm((N_EVEN, FILTER_ORDER), 0.01),
        'e_hy_bias': nrm((N_EVEN, H), 0.1),
        'e_w_out': nrm((N_EVEN, D, D), BETA * D ** -0.5),
        'o_w_pw1': nrm((N_ODD, D, 2 * D), D ** -0.5),
        'o_b_pw1': nrm((N_ODD, 2 * D), 0.02),
        'o_dw_w': nrm((N_ODD, CONF_K, D), CONF_K ** -0.5),
        'o_dw_b': nrm((N_ODD, D), 0.02),
        'o_ln_g': 1.0 + nrm((N_ODD, D), 0.02),
        'o_ln_b': nrm((N_ODD, D), 0.02),
        'o_w_pw2': nrm((N_ODD, D, D), BETA * D ** -0.5),
        'o_b_pw2': nrm((N_ODD, D), 0.02),
        'ln1_g': 1.0 + nrm((DEPTH, D), 0.02),
        'ln1_b': nrm((DEPTH, D), 0.02),
        'mlp_w1': nrm((DEPTH, D, D_FF), D ** -0.5),
        'mlp_w2': nrm((DEPTH, D_FF, D), BETA * D_FF ** -0.5),
        'ln2_g': 1.0 + nrm((DEPTH, D), 0.02),
        'ln2_b': nrm((DEPTH, D), 0.02),
    }


def reference(x_prompt, x_sample, e_w_in, e_conv_a, e_short_w, e_short_b, e_flt_w1, e_flt_b1,
              e_flt_w2, e_flt_b2, e_flt_w3, e_flt_b3, e_flt_w4, e_flt_freq, e_hy_bias, e_w_out,
              o_w_pw1, o_b_pw1, o_dw_w, o_dw_b, o_ln_g, o_ln_b, o_w_pw2, o_b_pw2,
              ln1_g, ln1_b, mlp_w1, mlp_w2, ln2_g, ln2_b):
    def run(x):
        for i in range(DEPTH):
            j = i // 2
            if i % 2 == 0:
                m = even_mixer(x, e_w_in[j], e_conv_a[j], e_short_w[j], e_short_b[j],
                               e_flt_w1[j], e_flt_b1[j], e_flt_w2[j], e_flt_b2[j],
                               e_flt_w3[j], e_flt_b3[j], e_flt_w4[j], e_flt_freq[j],
                               e_hy_bias[j], e_w_out[j])
            else:
                m = conformer_conv(x, o_w_pw1[j], o_b_pw1[j], o_dw_w[j], o_dw_b[j],
                                   o_ln_g[j], o_ln_b[j], o_w_pw2[j], o_b_pw2[j])
            x = layer_norm(ALPHA * x + m, ln1_g[i], ln1_b[i])
            x = layer_norm(ALPHA * x + sq_relu_mlp(x, mlp_w1[i], mlp_w2[i]), ln2_g[i], ln2_b[i])
        return x

    y_prompt = run(x_prompt)
    y_sample = run(x_sample)
    return (y_prompt, y_sample)
```

```python
import functools
import math

import jax
import jax.numpy as jnp
from jax import lax
from jax.experimental import pallas as pl
from jax.experimental.pallas import tpu as pltpu

F32 = jnp.float32
BF16 = jnp.bfloat16

SHORT_K = 3
CONF_K = 31
FILTER_BANDS = 16
FILTER_EMB = 1 + 2 * FILTER_BANDS
DECAY_TARGET = 1e-2
FAST_DECAY_PCT = 0.3
SLOW_DECAY_PCT = 1.5
LN_EPS = 1e-5

LANES = 128
SUBLANES = 8
MXU_DIM = 256

TOKEN_TILE = 512
CONF_HALO = 16
CONV_ROWS = 128
FF_CHUNK = 2048
E1_CHUNK = 128
E2_CHANNELS = 8
VMEM_LIMIT = 58 * 1024 * 1024


def _const_spec(shape):
    nd = len(shape)
    return pl.BlockSpec(shape, lambda *_: (0,) * nd, pipeline_mode=pl.Buffered(1))


def _layer_norm(x, g, b):
    mu = jnp.mean(x, axis=-1, keepdims=True)
    xc = x - mu
    var = jnp.mean(xc * xc, axis=-1, keepdims=True)
    return xc * lax.rsqrt(var + LN_EPS) * g + b


def _mlp_tail(x1, alpha, w1_ref, w2_ref, g2, b2):
    xb = x1.astype(BF16)
    d_ff = w1_ref.shape[1]
    y = None
    for c0 in range(0, d_ff, FF_CHUNK):
        h = jnp.dot(xb, w1_ref[:, c0:c0 + FF_CHUNK], preferred_element_type=F32)
        h = jnp.maximum(h, 0.0)
        h = (h * h).astype(BF16)
        part = jnp.dot(h, w2_ref[c0:c0 + FF_CHUNK, :], preferred_element_type=F32)
        y = part if y is None else y + part
    return _layer_norm(alpha * x1 + y, g2, b2)


def _filter_kernel(z_ref, trow_ref, w1_ref, b1_ref, w2_ref, b2_ref, w3_ref, b3_ref, w4f_ref, w4b_ref,
                   freq_ref, delta_ref, bias_ref, o_ref):
    hp = lax.Precision.HIGHEST
    zt = z_ref[...]
    lane = lax.broadcasted_iota(jnp.int32, (1, zt.shape[1]), 1)
    z = jnp.where(lane == 0, zt,
                  jnp.where(lane <= FILTER_BANDS, jnp.cos(zt),
                            jnp.where(lane <= 2 * FILTER_BANDS, -jnp.sin(zt), 0.0)))
    fr = freq_ref[0]
    h = jnp.sin(fr * (jnp.dot(z, w1_ref[0], precision=hp, preferred_element_type=F32) + b1_ref[0]))
    h = jnp.sin(fr * (jnp.dot(h, w2_ref[0], precision=hp, preferred_element_type=F32) + b2_ref[0]))
    h = jnp.sin(fr * (jnp.dot(h, w3_ref[0], precision=hp, preferred_element_type=F32) + b3_ref[0]))
    nt = (((1,), (1,)), ((), ()))
    kf = lax.dot_general(w4f_ref[0], h, nt, precision=hp, preferred_element_type=F32)
    kb = lax.dot_general(w4b_ref[0], h, nt, precision=hp, preferred_element_type=F32)
    n_taps = kf.shape[1]
    q = lax.broadcasted_iota(jnp.int32, (1, n_taps), 1)
    decay = jnp.exp(-trow_ref[...] * delta_ref[...])
    k = jnp.where(q >= n_taps // 2, kf, kb) * decay
    k = jnp.where(q == 0, 0.0, k)
    k = k / jnp.sum(jnp.abs(k), axis=1, keepdims=True)
    o_ref[0] = jnp.where(q == n_taps // 2, k + bias_ref[0], k)


def _hyena_filters(seq_len, flt_w1, flt_b1, flt_w2, flt_b2, flt_w3, flt_b3, flt_w4, flt_freq, hy_bias):
    n_layers, _, order = flt_w1.shape
    half = flt_w4.shape[2] // 2
    n_taps = 2 * seq_len
    t = jnp.linspace(0.0, 1.0, seq_len, dtype=F32)
    w = (2.0 * math.pi / seq_len) * jnp.arange(seq_len, dtype=F32)
    bands = jnp.linspace(1e-4, FILTER_BANDS - 1, FILTER_BANDS, dtype=F32)
    pos = jnp.abs(jnp.arange(n_taps) - seq_len) % seq_len
    ang = (bands[None, :] * w[:, None])[pos]
    ztab = jnp.concatenate([t[pos][:, None], ang, ang,
                            jnp.zeros((n_taps, LANES - FILTER_EMB), F32)], axis=1)
    trow = t[pos][None, :]
    max_decay = math.log(DECAY_TARGET) / FAST_DECAY_PCT
    min_decay = math.log(DECAY_TARGET) / SLOW_DECAY_PCT
    deltas = jnp.abs(jnp.linspace(min_decay, max_decay, half, dtype=F32))[:, None]
    w1p = jnp.pad(flt_w1.astype(F32), ((0, 0), (0, LANES - FILTER_EMB), (0, 0)))
    w4t = jnp.swapaxes(flt_w4.astype(F32), 1, 2)
    row = lambda a: a.astype(F32)[:, None, :]
    cc = LANES
    n_cc = half // cc
    lay = lambda shape: pl.BlockSpec((1,) + shape, lambda j, c: (j, 0, 0))
    return pl.pallas_call(
        _filter_kernel,
        grid=(n_layers, n_cc),
        in_specs=[
            _const_spec((n_taps, LANES)), _const_spec((1, n_taps)),
            lay((LANES, order)), lay((1, order)), lay((order, order)), lay((1, order)),
            lay((order, order)), lay((1, order)),
            pl.BlockSpec((1, cc, order), lambda j, c: (j, c, 0)),
            pl.BlockSpec((1, cc, order), lambda j, c: (j, n_cc + c, 0)),
            lay((1, order)),
            pl.BlockSpec((cc, 1), lambda j, c: (c, 0)),
            pl.BlockSpec((1, cc, 1), lambda j, c: (j, c, 0)),
        ],
        out_specs=pl.BlockSpec((1, cc, n_taps), lambda j, c: (j, c, 0)),
        out_shape=jax.ShapeDtypeStruct((n_layers, half, n_taps), F32),
        compiler_params=pltpu.CompilerParams(dimension_semantics=("arbitrary", "arbitrary"),
                                             vmem_limit_bytes=VMEM_LIMIT),
        name="hyena_filter",
    )(ztab, trow, w1p, row(flt_b1), flt_w2.astype(F32), row(flt_b2), flt_w3.astype(F32), row(flt_b3),
      w4t, w4t, row(flt_freq), deltas, hy_bias.astype(F32)[:, :, None])


def _even_in_kernel(x_ref, w_ref, p_ref, ya_ref, u_ref, x0_ref):
    xb = x_ref[0].astype(BF16)
    nt = (((1,), (1,)), ((), ()))
    pt = lax.dot_general(w_ref[0], xb, nt, preferred_element_type=F32)
    cc = pt.shape[0] // 6
    seq = pt.shape[1]
    a_b, a_c, a_h, p0, p1, pv = [pt[i * cc:(i + 1) * cc] for i in range(6)]
    prm = p_ref[0]
    col = lambda j: prm[:, j:j + 1]
    lane = lax.broadcasted_iota(jnp.int32, (1, seq), 1)

    def conv3(z, j):
        zp = jnp.where(lane == 0, 0.0, pltpu.roll(z, 1, 1))
        zn = jnp.where(lane == seq - 1, 0.0, pltpu.roll(z, seq - 1, 1))
        return col(j) * zp + col(j + 1) * z + col(j + 2) * zn

    ya_ref[0] = a_b * conv3(a_c * a_h, 0)
    x0_ref[0] = conv3(p0, 3) + col(12)
    x1 = conv3(p1, 6) + col(13)
    v = conv3(pv, 9) + col(14)
    u_ref[0] = x1 * v


def _even_in(x, w_chunks, prm):
    n_b, seq, d = x.shape
    n_cc, rows, _ = w_chunks.shape
    cc = rows // 6
    half = n_cc * cc
    out = jax.ShapeDtypeStruct((n_b, half, seq), F32)
    out_spec = pl.BlockSpec((1, cc, seq), lambda b, c: (b, c, 0))
    return pl.pallas_call(
        _even_in_kernel,
        grid=(n_b, n_cc),
        in_specs=[
            pl.BlockSpec((1, seq, d), lambda b, c: (b, 0, 0)),
            pl.BlockSpec((1, rows, d), lambda b, c: (c, 0, 0)),
            pl.BlockSpec((1, cc, 16), lambda b, c: (c, 0, 0)),
        ],
        out_specs=[out_spec, out_spec, out_spec],
        out_shape=[out, out, out],
        compiler_params=pltpu.CompilerParams(dimension_semantics=("arbitrary", "arbitrary"),
                                             vmem_limit_bytes=VMEM_LIMIT),
        name="even_in",
    )(x, w_chunks, prm)


def _long_conv_kernel(kk_ref, u_ref, x0_ref, o_ref, toep_ref):
    n_b, n_ch, seq = u_ref.shape
    blk = MXU_DIM
    n_blk = seq // blk

    def per_channel(c, carry):
        taps = kk_ref[pl.ds(c, 1), :]
        toep = pltpu.roll(jnp.broadcast_to(taps, (blk, 2 * seq)), 0, 1, stride=1, stride_axis=0)
        toep_ref[...] = toep.astype(BF16)
        u = u_ref[:, c, :].astype(BF16)
        ustack = jnp.concatenate([u[:, j * blk:(j + 1) * blk] for j in range(n_blk)], axis=0)
        acc = [None] * n_blk
        for dd in range(-(n_blk - 1), n_blk):
            j0, j1 = max(0, -dd), min(n_blk, n_blk - dd)
            t_dd = toep_ref[:, pl.ds(seq + blk * dd, blk)]
            r = jnp.dot(ustack[j0 * n_b:j1 * n_b], t_dd, preferred_element_type=F32)
            for n, j in enumerate(range(j0, j1)):
                part = r[n * n_b:(n + 1) * n_b]
                acc[j + dd] = part if acc[j + dd] is None else acc[j + dd] + part
        y = jnp.concatenate(acc, axis=1)
        o_ref[:, c, :] = x0_ref[:, c, :] * y
        return carry

    lax.fori_loop(0, n_ch, per_channel, 0)


def _long_conv(kk, u_t, x0_t):
    n_b, half, seq = u_t.shape
    cc = E2_CHANNELS
    blk_spec = pl.BlockSpec((n_b, cc, seq), lambda c: (0, c, 0))
    return pl.pallas_call(
        _long_conv_kernel,
        grid=(half // cc,),
        in_specs=[pl.BlockSpec((cc, 2 * seq), lambda c: (c, 0)), blk_spec, blk_spec],
        out_specs=blk_spec,
        out_shape=jax.ShapeDtypeStruct((n_b, half, seq), F32),
        scratch_shapes=[pltpu.VMEM((MXU_DIM, 2 * seq), BF16)],
        compiler_params=pltpu.CompilerParams(dimension_semantics=("arbitrary",),
                                             vmem_limit_bytes=VMEM_LIMIT),
        name="long_conv",
    )(kk, u_t, x0_t)


def _even_out_kernel(alpha, x_ref, ya_ref, yb_ref, woa_ref, wob_ref, g1_ref, b1_ref, w1_ref, w2_ref,
                     g2_ref, b2_ref, o_ref):
    tn = (((0,), (0,)), ((), ()))
    m = lax.dot_general(ya_ref[0].astype(BF16), woa_ref[...], tn, preferred_element_type=F32)
    m = m + lax.dot_general(yb_ref[0].astype(BF16), wob_ref[...], tn, preferred_element_type=F32)
    x1 = _layer_norm(alpha * x_ref[0] + m, g1_ref[...], b1_ref[...])
    o_ref[0] = _mlp_tail(x1, alpha, w1_ref, w2_ref, g2_ref[...], b2_ref[...])


def _even_out(alpha, x, ya_t, yb_t, wo_a, wo_b, g1, b1, w1, w2, g2, b2):
    n_b, seq, d = x.shape
    half = ya_t.shape[1]
    tm = TOKEN_TILE
    tok = pl.BlockSpec((1, tm, d), lambda b, t: (b, t, 0))
    chan = pl.BlockSpec((1, half, tm), lambda b, t: (b, 0, t))
    return pl.pallas_call(
        functools.partial(_even_out_kernel, alpha),
        grid=(n_b, seq // tm),
        in_specs=[tok, chan, chan, _const_spec(wo_a.shape), _const_spec(wo_b.shape),
                  _const_spec(g1.shape), _const_spec(b1.shape), _const_spec(w1.shape), _const_spec(w2.shape),
                  _const_spec(g2.shape), _const_spec(b2.shape)],
        out_specs=tok,
        out_shape=jax.ShapeDtypeStruct(x.shape, F32),
        compiler_params=pltpu.CompilerParams(dimension_semantics=("arbitrary", "arbitrary"),
                                             vmem_limit_bytes=VMEM_LIMIT),
        name="even_out",
    )(x, ya_t, yb_t, wo_a, wo_b, g1, b1, w1, w2, g2, b2)


def _odd_kernel(alpha, seq, x_ref, xp_ref, xn_ref, wp1_ref, bp1_ref, dw_ref, dwb_ref, lg_ref, lb_ref,
                wp2_ref, bp2_ref, g1_ref, b1_ref, w1_ref, w2_ref, g2_ref, b2_ref, o_ref, glu_ref, cv_ref):
    tm, d = x_ref.shape[1], x_ref.shape[2]
    halo = xp_ref.shape[1]
    t = pl.program_id(1)
    xt = x_ref[0]
    xcat = jnp.concatenate([xp_ref[0], xt, xn_ref[0]], axis=0).astype(BF16)
    h = jnp.dot(xcat, wp1_ref[...], preferred_element_type=F32) + bp1_ref[...]
    glu = h[:, :d] * jax.nn.sigmoid(h[:, d:])
    pos = t * tm - halo + lax.broadcasted_iota(jnp.int32, (tm + 2 * halo, 1), 0)
    glu_ref[...] = jnp.where((pos >= 0) & (pos < seq), glu, 0.0)
    n_taps = CONF_K
    off = halo - n_taps // 2
    for l0 in range(0, d, LANES):
        lanes = pl.ds(l0, LANES)
        for r0 in range(0, tm, CONV_ROWS):
            acc = None
            for k in range(n_taps):
                term = dw_ref[pl.ds(k, 1), lanes] * glu_ref[pl.ds(r0 + off + k, CONV_ROWS), lanes]
                acc = term if acc is None else acc + term
            cv_ref[pl.ds(r0, CONV_ROWS), lanes] = acc
    c = _layer_norm(cv_ref[...] + dwb_ref[...], lg_ref[...], lb_ref[...])
    c = c * jax.nn.sigmoid(c)
    m = jnp.dot(c.astype(BF16), wp2_ref[...], preferred_element_type=F32) + bp2_ref[...]
    x1 = _layer_norm(alpha * xt + m, g1_ref[...], b1_ref[...])
    o_ref[0] = _mlp_tail(x1, alpha, w1_ref, w2_ref, g2_ref[...], b2_ref[...])


def _odd_layer(alpha, x, wp1, bp1, dw, dwb, lg, lb, wp2, bp2, g1, b1, w1, w2, g2, b2):
    n_b, seq, d = x.shape
    tm, halo = TOKEN_TILE, CONF_HALO
    per = tm // halo
    last = seq // halo - 1
    tok = pl.BlockSpec((1, tm, d), lambda b, t: (b, t, 0))
    prev = pl.BlockSpec((1, halo, d), lambda b, t: (b, jnp.maximum(t * per - 1, 0), 0))
    nxt = pl.BlockSpec((1, halo, d), lambda b, t: (b, jnp.minimum((t + 1) * per, last), 0))
    consts = [wp1, bp1, dw, dwb, lg, lb, wp2, bp2, g1, b1, w1, w2, g2, b2]
    return pl.pallas_call(
        functools.partial(_odd_kernel, alpha, seq),
        grid=(n_b, seq // tm),
        in_specs=[tok, prev, nxt] + [_const_spec(a.shape) for a in consts],
        out_specs=tok,
        out_shape=jax.ShapeDtypeStruct(x.shape, F32),
        scratch_shapes=[pltpu.VMEM((tm + 2 * halo, d), F32), pltpu.VMEM((tm, d), F32)],
        compiler_params=pltpu.CompilerParams(dimension_semantics=("arbitrary", "arbitrary"),
                                             vmem_limit_bytes=VMEM_LIMIT),
        name="odd_layer",
    )(x, x, x, *consts)


def kernel(x_prompt, x_sample, e_w_in, e_conv_a, e_short_w, e_short_b, e_flt_w1, e_flt_b1, e_flt_w2, e_flt_b2,
           e_flt_w3, e_flt_b3, e_flt_w4, e_flt_freq, e_hy_bias, e_w_out, o_w_pw1, o_b_pw1, o_dw_w, o_dw_b,
           o_ln_g, o_ln_b, o_w_pw2, o_b_pw2, ln1_g, ln1_b, mlp_w1, mlp_w2, ln2_g, ln2_b):
    depth = ln1_g.shape[0]
    alpha = (2 * depth) ** 0.25
    n_prompt = x_prompt.shape[0]
    seq, d = x_prompt.shape[1], x_prompt.shape[2]
    half = d // 2
    assert x_sample.shape[1:] == (seq, d)
    assert seq % TOKEN_TILE == 0 and seq % MXU_DIM == 0 and half % E1_CHUNK == 0 and half % E2_CHANNELS == 0
    assert CONF_HALO >= CONF_K // 2 and o_dw_w.shape[1] == CONF_K and e_conv_a.shape[1] == SHORT_K

    x = jnp.concatenate([x_prompt, x_sample], axis=0)
    row = lambda a: a.astype(F32)[None, :]

    kk_all = _hyena_filters(seq, e_flt_w1, e_flt_b1, e_flt_w2, e_flt_b2, e_flt_w3, e_flt_b3, e_flt_w4,
                            e_flt_freq, e_hy_bias)

    n_cc = half // E1_CHUNK
    for i in range(depth):
        j = i // 2
        w1 = mlp_w1[i].astype(BF16)
        w2 = mlp_w2[i].astype(BF16)
        g1, b1, g2, b2 = row(ln1_g[i]), row(ln1_b[i]), row(ln2_g[i]), row(ln2_b[i])
        if i % 2 == 0:
            w_t = e_w_in[j].T.reshape(6, n_cc, E1_CHUNK, d).transpose(1, 0, 2, 3).reshape(n_cc, 6 * E1_CHUNK, d)
            sw, sb = e_short_w[j], e_short_b[j]
            cols = [e_conv_a[j][k] for k in range(SHORT_K)]
            for g in range(3):
                cols += [sw[k, g * half:(g + 1) * half] for k in range(SHORT_K)]
            cols += [sb[g * half:(g + 1) * half] for g in range(3)]
            cols += [jnp.zeros((half,), F32)]
            prm = jnp.stack(cols, axis=1).astype(F32).reshape(n_cc, E1_CHUNK, 16)
            ya_t, u_t, x0_t = _even_in(x, w_t.astype(BF16), prm)
            yb_t = _long_conv(kk_all[j], u_t, x0_t)
            wo = e_w_out[j].astype(BF16)
            x = _even_out(alpha, x, ya_t, yb_t, wo[:half], wo[half:], g1, b1, w1, w2, g2, b2)
        else:
            dw = jnp.pad(o_dw_w[j].astype(F32), ((0, 32 - CONF_K), (0, 0)))
            x = _odd_layer(alpha, x, o_w_pw1[j].astype(BF16), row(o_b_pw1[j]), dw, row(o_dw_b[j]),
                           row(o_ln_g[j]), row(o_ln_b[j]), o_w_pw2[j].astype(BF16), row(o_b_pw2[j]),
                           g1, b1, w1, w2, g2, b2)
    return (x[:n_prompt], x[n_prompt:])
```

```python
import functools
import math

import jax
import jax.numpy as jnp
from jax import lax
from jax.experimental import pallas as pl
from jax.experimental.pallas import tpu as pltpu

F32 = jnp.float32
BF16 = jnp.bfloat16

SHORT_K = 3
CONF_K = 31
FILTER_BANDS = 16
FILTER_EMB = 1 + 2 * FILTER_BANDS
DECAY_TARGET = 1e-2
FAST_DECAY_PCT = 0.3
SLOW_DECAY_PCT = 1.5
LN_EPS = 1e-5

LANES = 128
SUBLANES = 8
MXU_DIM = 256

TOKEN_TILE = 512
CONF_HALO = 16
CONV_ROWS = 128
FF_CHUNK = 2048
E1_CHUNK = 128
E2_CHANNELS = 8
VMEM_LIMIT = 58 * 1024 * 1024


def _const_spec(shape):
    nd = len(shape)
    return pl.BlockSpec(shape, lambda *_: (0,) * nd, pipeline_mode=pl.Buffered(1))


def _routed_specs(parts, n_k, tail_block, tail_index):
    specs, start = [], 0
    for p in parts:
        n = p.shape[0]

        def index(b, k, start=start, n=n):
            pos = jnp.clip((b - start) * n_k + k, 0, n * n_k - 1)
            return (pos // n_k,) + tail_index(pos % n_k)

        specs.append(pl.BlockSpec((1,) + tail_block, index))
        start += n
    return specs


def _routed_read(refs, sizes, b):
    x, start = refs[0][0], sizes[0]
    for r, n in zip(refs[1:], sizes[1:]):
        x = jnp.where(b >= start, r[0], x)
        start += n
    return x


def _layer_norm(x, g, b):
    mu = jnp.mean(x, axis=-1, keepdims=True)
    xc = x - mu
    var = jnp.mean(xc * xc, axis=-1, keepdims=True)
    return xc * lax.rsqrt(var + LN_EPS) * g + b


def _mlp_tail(x1, alpha, w1_ref, w2_ref, g2, b2):
    xb = x1.astype(BF16)
    d_ff = w1_ref.shape[1]
    y = None
    for c0 in range(0, d_ff, FF_CHUNK):
        h = jnp.dot(xb, w1_ref[:, c0:c0 + FF_CHUNK], preferred_element_type=F32)
        h = jnp.maximum(h, 0.0)
        h = (h * h).astype(BF16)
        part = jnp.dot(h, w2_ref[c0:c0 + FF_CHUNK, :], preferred_element_type=F32)
        y = part if y is None else y + part
    return _layer_norm(alpha * x1 + y, g2, b2)


def _filter_kernel(z_ref, trow_ref, w1_ref, b1_ref, w2_ref, b2_ref, w3_ref, b3_ref, w4f_ref, w4b_ref,
                   freq_ref, delta_ref, bias_ref, o_ref, feat_ref, h_ref):
    hp = lax.Precision.HIGHEST

    @pl.when((pl.program_id(0) == 0) & (pl.program_id(1) == 0))
    def _():
        zt = z_ref[...]
        lane = lax.broadcasted_iota(jnp.int32, (1, zt.shape[1]), 1)
        feat_ref[...] = jnp.where(lane == 0, zt,
                                  jnp.where(lane <= FILTER_BANDS, jnp.cos(zt),
                                            jnp.where(lane <= 2 * FILTER_BANDS, -jnp.sin(zt), 0.0)))

    @pl.when(pl.program_id(1) == 0)
    def _():
        fr = freq_ref[0]
        h = jnp.sin(fr * (jnp.dot(feat_ref[...], w1_ref[0], precision=hp, preferred_element_type=F32) + b1_ref[0]))
        h = jnp.sin(fr * (jnp.dot(h, w2_ref[0], precision=hp, preferred_element_type=F32) + b2_ref[0]))
        h_ref[...] = jnp.sin(fr * (jnp.dot(h, w3_ref[0], precision=hp, preferred_element_type=F32) + b3_ref[0]))

    h = h_ref[...]
    nt = (((1,), (1,)), ((), ()))
    kf = lax.dot_general(w4f_ref[0], h, nt, precision=hp, preferred_element_type=F32)
    kb = lax.dot_general(w4b_ref[0], h, nt, precision=hp, preferred_element_type=F32)
    n_taps = kf.shape[1]
    q = lax.broadcasted_iota(jnp.int32, (1, n_taps), 1)
    decay = jnp.exp(-trow_ref[...] * delta_ref[...])
    k = jnp.where(q >= n_taps // 2, kf, kb) * decay
    k = jnp.where(q == 0, 0.0, k)
    k = k / jnp.sum(jnp.abs(k), axis=1, keepdims=True)
    o_ref[0] = jnp.where(q == n_taps // 2, k + bias_ref[0], k)


def _hyena_filters(seq_len, flt_w1, flt_b1, flt_w2, flt_b2, flt_w3, flt_b3, flt_w4, flt_freq, hy_bias):
    n_layers, _, order = flt_w1.shape
    half = flt_w4.shape[2] // 2
    n_taps = 2 * seq_len
    t = jnp.linspace(0.0, 1.0, seq_len, dtype=F32)
    w = (2.0 * math.pi / seq_len) * jnp.arange(seq_len, dtype=F32)
    bands = jnp.linspace(1e-4, FILTER_BANDS - 1, FILTER_BANDS, dtype=F32)
    pos = jnp.abs(jnp.arange(n_taps) - seq_len) % seq_len
    ang = (bands[None, :] * w[:, None])[pos]
    ztab = jnp.concatenate([t[pos][:, None], ang, ang,
                            jnp.zeros((n_taps, LANES - FILTER_EMB), F32)], axis=1)
    trow = t[pos][None, :]
    max_decay = math.log(DECAY_TARGET) / FAST_DECAY_PCT
    min_decay = math.log(DECAY_TARGET) / SLOW_DECAY_PCT
    deltas = jnp.abs(jnp.linspace(min_decay, max_decay, half, dtype=F32))[:, None]
    w1p = jnp.pad(flt_w1.astype(F32), ((0, 0), (0, LANES - FILTER_EMB), (0, 0)))
    w4t = jnp.swapaxes(flt_w4.astype(F32), 1, 2)
    row = lambda a: a.astype(F32)[:, None, :]
    cc = LANES
    n_cc = half // cc
    lay = lambda shape: pl.BlockSpec((1,) + shape, lambda j, c: (j, 0, 0))
    return pl.pallas_call(
        _filter_kernel,
        grid=(n_layers, n_cc),
        in_specs=[
            _const_spec((n_taps, LANES)), _const_spec((1, n_taps)),
            lay((LANES, order)), lay((1, order)), lay((order, order)), lay((1, order)),
            lay((order, order)), lay((1, order)),
            pl.BlockSpec((1, cc, order), lambda j, c: (j, c, 0)),
            pl.BlockSpec((1, cc, order), lambda j, c: (j, n_cc + c, 0)),
            lay((1, order)),
            pl.BlockSpec((cc, 1), lambda j, c: (c, 0)),
            pl.BlockSpec((1, cc, 1), lambda j, c: (j, c, 0)),
        ],
        out_specs=pl.BlockSpec((1, cc, n_taps), lambda j, c: (j, c, 0)),
        out_shape=jax.ShapeDtypeStruct((n_layers, half, n_taps), F32),
        scratch_shapes=[pltpu.VMEM((n_taps, LANES), F32), pltpu.VMEM((n_taps, order), F32)],
        compiler_params=pltpu.CompilerParams(dimension_semantics=("arbitrary", "arbitrary"),
                                             vmem_limit_bytes=VMEM_LIMIT),
        name="hyena_filter",
    )(ztab, trow, w1p, row(flt_b1), flt_w2.astype(F32), row(flt_b2), flt_w3.astype(F32), row(flt_b3),
      w4t, w4t, row(flt_freq), deltas, hy_bias.astype(F32)[:, :, None])


def _even_in_kernel(sizes, *refs):
    n_x = len(sizes)
    w_ref, p_ref, ya_ref, u_ref, x0_ref = refs[n_x:]
    xb = _routed_read(refs[:n_x], sizes, pl.program_id(0)).astype(BF16)
    nt = (((1,), (1,)), ((), ()))
    pt = lax.dot_general(w_ref[0], xb, nt, preferred_element_type=F32)
    cc = pt.shape[0] // 6
    seq = pt.shape[1]
    a_b, a_c, a_h, p0, p1, pv = [pt[i * cc:(i + 1) * cc] for i in range(6)]
    prm = p_ref[0]
    col = lambda j: prm[:, j:j + 1]
    lane = lax.broadcasted_iota(jnp.int32, (1, seq), 1)

    def conv3(z, j):
        zp = jnp.where(lane == 0, 0.0, pltpu.roll(z, 1, 1))
        zn = jnp.where(lane == seq - 1, 0.0, pltpu.roll(z, seq - 1, 1))
        return col(j) * zp + col(j + 1) * z + col(j + 2) * zn

    ya_ref[0] = a_b * conv3(a_c * a_h, 0)
    x0_ref[0] = conv3(p0, 3) + col(12)
    x1 = conv3(p1, 6) + col(13)
    v = conv3(pv, 9) + col(14)
    u_ref[0] = x1 * v


def _even_in(x_parts, w_chunks, prm):
    sizes = tuple(p.shape[0] for p in x_parts)
    n_b = sum(sizes)
    _, seq, d = x_parts[0].shape
    n_cc, rows, _ = w_chunks.shape
    cc = rows // 6
    half = n_cc * cc
    out = jax.ShapeDtypeStruct((n_b, half, seq), F32)
    out_spec = pl.BlockSpec((1, cc, seq), lambda b, c: (b, c, 0))
    return pl.pallas_call(
        functools.partial(_even_in_kernel, sizes),
        grid=(n_b, n_cc),
        in_specs=_routed_specs(x_parts, n_cc, (seq, d), lambda c: (0, 0)) + [
            pl.BlockSpec((1, rows, d), lambda b, c: (c, 0, 0)),
            pl.BlockSpec((1, cc, 16), lambda b, c: (c, 0, 0)),
        ],
        out_specs=[out_spec, out_spec, out_spec],
        out_shape=[out, out, out],
        compiler_params=pltpu.CompilerParams(dimension_semantics=("arbitrary", "arbitrary"),
                                             vmem_limit_bytes=VMEM_LIMIT),
        name="even_in",
    )(*x_parts, w_chunks, prm)


def _long_conv_kernel(kk_ref, u_ref, o_ref, toep_ref, us_ref, yc_ref):
    n_b, n_ch, seq = u_ref.shape
    blk = MXU_DIM
    n_blk = seq // blk

    uc = pltpu.einshape("bcl->cbl", u_ref[...]).astype(BF16)
    for j in range(n_blk):
        us_ref[:, pl.ds(j * n_b, n_b), :] = uc[:, :, j * blk:(j + 1) * blk]

    def per_channel(c, carry):
        taps = kk_ref[pl.ds(c, 1), :]
        toep = pltpu.roll(jnp.broadcast_to(taps, (blk, 2 * seq)), 0, 1, stride=1, stride_axis=0)
        toep_ref[...] = toep.astype(BF16)
        acc = [None] * n_blk
        for dd in range(-(n_blk - 1), n_blk):
            j0, j1 = max(0, -dd), min(n_blk, n_blk - dd)
            t_dd = toep_ref[:, pl.ds(seq + blk * dd, blk)]
            r = jnp.dot(us_ref[c, pl.ds(j0 * n_b, (j1 - j0) * n_b), :], t_dd, preferred_element_type=F32)
            for n, j in enumerate(range(j0, j1)):
                part = r[n * n_b:(n + 1) * n_b]
                acc[j + dd] = part if acc[j + dd] is None else acc[j + dd] + part
        for i in range(n_blk):
            yc_ref[c, :, pl.ds(i * blk, blk)] = acc[i]
        return carry

    lax.fori_loop(0, n_ch, per_channel, 0)
    o_ref[...] = pltpu.einshape("cbl->bcl", yc_ref[...])


def _long_conv(kk, u_t):
    n_b, half, seq = u_t.shape
    cc = E2_CHANNELS
    blk_spec = pl.BlockSpec((n_b, cc, seq), lambda c: (0, c, 0))
    return pl.pallas_call(
        _long_conv_kernel,
        grid=(half // cc,),
        in_specs=[pl.BlockSpec((cc, 2 * seq), lambda c: (c, 0)), blk_spec],
        out_specs=blk_spec,
        out_shape=jax.ShapeDtypeStruct((n_b, half, seq), F32),
        scratch_shapes=[pltpu.VMEM((MXU_DIM, 2 * seq), BF16),
                        pltpu.VMEM((cc, (seq // MXU_DIM) * n_b, MXU_DIM), BF16),
                        pltpu.VMEM((cc, n_b, seq), F32)],
        compiler_params=pltpu.CompilerParams(dimension_semantics=("arbitrary",),
                                             vmem_limit_bytes=VMEM_LIMIT),
        name="long_conv",
    )(kk, u_t)


def _even_out_kernel(alpha, sizes, *refs):
    n_x = len(sizes)
    ya_ref, cv_ref, x0_ref, wo_ref, g1_ref, b1_ref, w1_ref, w2_ref, g2_ref, b2_ref, o_ref = refs[n_x:]
    x = _routed_read(refs[:n_x], sizes, pl.program_id(0))
    ycat = jnp.concatenate([ya_ref[0], x0_ref[0] * cv_ref[0]], axis=0).astype(BF16)
    m = lax.dot_general(ycat, wo_ref[...], (((0,), (0,)), ((), ())), preferred_element_type=F32)
    x1 = _layer_norm(alpha * x + m, g1_ref[...], b1_ref[...])
    o_ref[0] = _mlp_tail(x1, alpha, w1_ref, w2_ref, g2_ref[...], b2_ref[...])


def _even_out(alpha, x_parts, ya_t, cv_t, x0_t, wo, g1, b1, w1, w2, g2, b2):
    sizes = tuple(p.shape[0] for p in x_parts)
    n_b = sum(sizes)
    _, seq, d = x_parts[0].shape
    half = ya_t.shape[1]
    tm = TOKEN_TILE
    tok = pl.BlockSpec((1, tm, d), lambda b, t: (b, t, 0))
    chan = pl.BlockSpec((1, half, tm), lambda b, t: (b, 0, t))
    consts = [wo, g1, b1, w1, w2, g2, b2]
    return pl.pallas_call(
        functools.partial(_even_out_kernel, alpha, sizes),
        grid=(n_b, seq // tm),
        in_specs=(_routed_specs(x_parts, seq // tm, (tm, d), lambda t: (t, 0)) + [chan, chan, chan]
                  + [_const_spec(a.shape) for a in consts]),
        out_specs=tok,
        out_shape=jax.ShapeDtypeStruct((n_b, seq, d), F32),
        compiler_params=pltpu.CompilerParams(dimension_semantics=("arbitrary", "arbitrary"),
                                             vmem_limit_bytes=VMEM_LIMIT),
        name="even_out",
    )(*x_parts, ya_t, cv_t, x0_t, *consts)


def _conv_shift_rows(tm, halo):
    off = halo - CONF_K // 2
    return tm + SUBLANES * ((off + CONF_K - 1) // SUBLANES)


def _odd_kernel(alpha, seq, n_tiles, out_sizes, x_ref, xp_ref, xn_ref, wp1_ref, bp1_ref, dw_ref, dwb_ref, lg_ref,
                lb_ref, wp2_ref, bp2_ref, g1_ref, b1_ref, w1_ref, w2_ref, g2_ref, b2_ref, *rest):
    out_refs = rest[:len(out_sizes)]
    glu_ref, gs_ref, cv_ref, xs_ref, x1_ref, xb_ref, y_ref = rest[len(out_sizes):]
    tm, d = x_ref.shape[1], x_ref.shape[2]
    halo = xp_ref.shape[1]
    n_lt = d // LANES
    s = pl.program_id(0)

    @pl.when(s == 0)
    def _():
        cv_ref[...] = jnp.zeros_like(cv_ref)
        xs_ref[...] = jnp.zeros_like(xs_ref)

    cv = jnp.concatenate([cv_ref[j] for j in range(n_lt)], axis=1)
    c = _layer_norm(cv + dwb_ref[...], lg_ref[...], lb_ref[...])
    c = c * jax.nn.sigmoid(c)
    m = jnp.dot(c.astype(BF16), wp2_ref[...], preferred_element_type=F32) + bp2_ref[...]
    x1 = _layer_norm(alpha * xs_ref[...] + m, g1_ref[...], b1_ref[...])
    x1_ref[...] = x1
    xb_ref[...] = x1.astype(BF16)
    y_ref[...] = jnp.zeros_like(y_ref)

    t = jnp.minimum(s, n_tiles - 1) % (seq // tm)
    xt = x_ref[0]
    xcat = jnp.concatenate([xp_ref[0], xt, xn_ref[0]], axis=0).astype(BF16)
    h = jnp.dot(xcat, wp1_ref[...], preferred_element_type=F32) + bp1_ref[...]
    glu = h[:, :d] * jax.nn.sigmoid(h[:, d:])
    pos = t * tm - halo + lax.broadcasted_iota(jnp.int32, (tm + 2 * halo, 1), 0)
    glu = jnp.where((pos >= 0) & (pos < seq), glu, 0.0)
    for j in range(n_lt):
        glu_ref[j] = glu[:, j * LANES:(j + 1) * LANES]
    xs_ref[...] = xt

    off = halo - CONF_K // 2
    n_shift = _conv_shift_rows(tm, halo)

    def chunk(i, carry):
        hid = jnp.dot(xb_ref[...], w1_ref[i], preferred_element_type=F32)
        hid = jnp.maximum(hid, 0.0)
        hid = (hid * hid).astype(BF16)
        y_ref[...] += jnp.dot(hid, w2_ref[i], preferred_element_type=F32)
        for r in range(1, SUBLANES):
            gs_ref[r - 1] = glu_ref[i, pl.ds(r, n_shift), :]
        for r0 in range(0, tm, CONV_ROWS):
            acc = None
            for k in range(CONF_K):
                a8, r = SUBLANES * ((off + k) // SUBLANES), (off + k) % SUBLANES
                if r == 0:
                    g = glu_ref[i, pl.ds(r0 + a8, CONV_ROWS), :]
                else:
                    g = gs_ref[r - 1, pl.ds(r0 + a8, CONV_ROWS), :]
                term = dw_ref[i, pl.ds(k, 1), :] * g
                acc = term if acc is None else acc + term
            cv_ref[i, pl.ds(r0, CONV_ROWS), :] = acc
        return carry

    lax.fori_loop(0, n_lt, chunk, 0)
    y_ref[...] = _layer_norm(alpha * x1_ref[...] + y_ref[...], g2_ref[...], b2_ref[...])
    b_back = jnp.maximum(s - 1, 0) // (seq // tm)
    start = 0
    for o_ref, n_rows in zip(out_refs, out_sizes):
        @pl.when((b_back >= start) & (b_back < start + n_rows))
        def _(o_ref=o_ref):
            o_ref[0] = y_ref[...]
        start += n_rows


def _odd_layer(alpha, x, wp1, bp1, dw, dwb, lg, lb, wp2, bp2, g1, b1, w1, w2, g2, b2, out_sizes=None):
    n_b, seq, d = x.shape
    out_sizes = (n_b,) if out_sizes is None else tuple(out_sizes)
    assert sum(out_sizes) == n_b
    tm, halo = TOKEN_TILE, CONF_HALO
    per = tm // halo
    last = seq // halo - 1
    n_t = seq // tm
    n_tiles = n_b * n_t
    front = lambda s: jnp.minimum(s, n_tiles - 1)
    back = lambda s: jnp.maximum(s - 1, 0)
    tok = pl.BlockSpec((1, tm, d), lambda s: (front(s) // n_t, front(s) % n_t, 0))
    prev = pl.BlockSpec((1, halo, d), lambda s: (front(s) // n_t, jnp.maximum((front(s) % n_t) * per - 1, 0), 0))
    nxt = pl.BlockSpec((1, halo, d), lambda s: (front(s) // n_t, jnp.minimum((front(s) % n_t + 1) * per, last), 0))

    def out_spec(start, n_rows):
        def index(s):
            tile = jnp.clip(back(s) - start * n_t, 0, n_rows * n_t - 1)
            return (tile // n_t, tile % n_t, 0)
        return pl.BlockSpec((1, tm, d), index)

    starts = [sum(out_sizes[:i]) for i in range(len(out_sizes))]
    n_lt = d // LANES
    d_ff = w1.shape[1]
    fc = d_ff // n_lt
    dw_t = dw.reshape(dw.shape[0], n_lt, LANES).transpose(1, 0, 2)
    w1_c = w1.reshape(d, n_lt, fc).transpose(1, 0, 2)
    w2_c = w2.reshape(n_lt, fc, d)
    consts = [wp1, bp1, dw_t, dwb, lg, lb, wp2, bp2, g1, b1, w1_c, w2_c, g2, b2]
    return pl.pallas_call(
        functools.partial(_odd_kernel, alpha, seq, n_tiles, out_sizes),
        grid=(n_tiles + 1,),
        in_specs=[tok, prev, nxt] + [_const_spec(a.shape) for a in consts],
        out_specs=[out_spec(st, nr) for st, nr in zip(starts, out_sizes)],
        out_shape=[jax.ShapeDtypeStruct((nr, seq, d), F32) for nr in out_sizes],
        scratch_shapes=[pltpu.VMEM((n_lt, tm + 2 * halo, LANES), F32),
                        pltpu.VMEM((SUBLANES - 1, _conv_shift_rows(tm, halo), LANES), F32),
                        pltpu.VMEM((n_lt, tm, LANES), F32), pltpu.VMEM((tm, d), F32),
                        pltpu.VMEM((tm, d), F32), pltpu.VMEM((tm, d), BF16), pltpu.VMEM((tm, d), F32)],
        compiler_params=pltpu.CompilerParams(dimension_semantics=("arbitrary",),
                                             vmem_limit_bytes=VMEM_LIMIT),
        name="odd_layer",
    )(x, x, x, *consts)


def kernel(x_prompt, x_sample, e_w_in, e_conv_a, e_short_w, e_short_b, e_flt_w1, e_flt_b1, e_flt_w2, e_flt_b2,
           e_flt_w3, e_flt_b3, e_flt_w4, e_flt_freq, e_hy_bias, e_w_out, o_w_pw1, o_b_pw1, o_dw_w, o_dw_b,
           o_ln_g, o_ln_b, o_w_pw2, o_b_pw2, ln1_g, ln1_b, mlp_w1, mlp_w2, ln2_g, ln2_b):
    depth = ln1_g.shape[0]
    alpha = (2 * depth) ** 0.25
    n_prompt = x_prompt.shape[0]
    seq, d = x_prompt.shape[1], x_prompt.shape[2]
    half = d // 2
    assert x_sample.shape[1:] == (seq, d)
    assert seq % TOKEN_TILE == 0 and seq % MXU_DIM == 0 and half % E1_CHUNK == 0 and half % E2_CHANNELS == 0
    assert CONF_HALO >= CONF_K // 2 and o_dw_w.shape[1] == CONF_K and e_conv_a.shape[1] == SHORT_K

    x_parts = (x_prompt, x_sample)
    row = lambda a: a.astype(F32)[None, :]

    kk_all = _hyena_filters(seq, e_flt_w1, e_flt_b1, e_flt_w2, e_flt_b2, e_flt_w3, e_flt_b3, e_flt_w4,
                            e_flt_freq, e_hy_bias)

    n_cc = half // E1_CHUNK
    for i in range(depth):
        j = i // 2
        w1 = mlp_w1[i].astype(BF16)
        w2 = mlp_w2[i].astype(BF16)
        g1, b1, g2, b2 = row(ln1_g[i]), row(ln1_b[i]), row(ln2_g[i]), row(ln2_b[i])
        if i % 2 == 0:
            w_t = e_w_in[j].T.reshape(6, n_cc, E1_CHUNK, d).transpose(1, 0, 2, 3).reshape(n_cc, 6 * E1_CHUNK, d)
            sw, sb = e_short_w[j], e_short_b[j]
            cols = [e_conv_a[j][k] for k in range(SHORT_K)]
            for g in range(3):
                cols += [sw[k, g * half:(g + 1) * half] for k in range(SHORT_K)]
            cols += [sb[g * half:(g + 1) * half] for g in range(3)]
            cols += [jnp.zeros((half,), F32)]
            prm = jnp.stack(cols, axis=1).astype(F32).reshape(n_cc, E1_CHUNK, 16)
            ya_t, u_t, x0_t = _even_in(x_parts, w_t.astype(BF16), prm)
            cv_t = _long_conv(kk_all[j], u_t)
            x_parts = (_even_out(alpha, x_parts, ya_t, cv_t, x0_t, e_w_out[j].astype(BF16),
                                 g1, b1, w1, w2, g2, b2),)
        else:
            dw = jnp.pad(o_dw_w[j].astype(F32), ((0, 32 - CONF_K), (0, 0)))
            (x,) = x_parts
            out_sizes = (n_prompt, x.shape[0] - n_prompt) if i == depth - 1 else None
            x_parts = tuple(_odd_layer(alpha, x, o_w_pw1[j].astype(BF16), row(o_b_pw1[j]), dw, row(o_dw_b[j]),
                                       row(o_ln_g[j]), row(o_ln_b[j]), o_w_pw2[j].astype(BF16), row(o_b_pw2[j]),
                                       g1, b1, w1, w2, g2, b2, out_sizes))
    if len(x_parts) == 1:
        return (x_parts[0][:n_prompt], x_parts[0][n_prompt:])
    return x_parts
```

```python
import functools
import math

import jax
import jax.numpy as jnp
import numpy as np
from jax import lax
from jax.experimental import pallas as pl
from jax.experimental.pallas import tpu as pltpu

F32 = jnp.float32
BF16 = jnp.bfloat16

SHORT_K = 3
CONF_K = 31
FILTER_BANDS = 16
FILTER_EMB = 1 + 2 * FILTER_BANDS
DECAY_TARGET = 1e-2
FAST_DECAY_PCT = 0.3
SLOW_DECAY_PCT = 1.5
LN_EPS = 1e-5

LANES = 128
SUBLANES = 8
MXU_DIM = 256

TOKEN_TILE = 512
CONF_HALO = 16
CONV_ROWS = 128
FF_CHUNK = 2048
E1_CHUNK = 128
E2_CHANNELS = 8
VMEM_LIMIT = 58 * 1024 * 1024


def _const_spec(shape):
    nd = len(shape)
    return pl.BlockSpec(shape, lambda *_: (0,) * nd, pipeline_mode=pl.Buffered(1))


def _routed_specs(parts, n_k, tail_block, tail_index):
    specs, start = [], 0
    for p in parts:
        n = p.shape[0]

        def index(b, k, start=start, n=n):
            pos = jnp.clip((b - start) * n_k + k, 0, n * n_k - 1)
            return (pos // n_k,) + tail_index(pos % n_k)

        specs.append(pl.BlockSpec((1,) + tail_block, index))
        start += n
    return specs


def _routed_read(refs, sizes, b):
    x, start = refs[0][0], sizes[0]
    for r, n in zip(refs[1:], sizes[1:]):
        x = jnp.where(b >= start, r[0], x)
        start += n
    return x


def _layer_norm(x, g, b):
    mu = jnp.mean(x, axis=-1, keepdims=True)
    xc = x - mu
    var = jnp.mean(xc * xc, axis=-1, keepdims=True)
    return xc * lax.rsqrt(var + LN_EPS) * g + b


def _mlp_tail(x1, alpha, w1_ref, w2_ref, g2, b2):
    xb = x1.astype(BF16)
    d_ff = w1_ref.shape[1]
    y = None
    for c0 in range(0, d_ff, FF_CHUNK):
        h = jnp.dot(xb, w1_ref[:, c0:c0 + FF_CHUNK], preferred_element_type=F32)
        h = jnp.maximum(h, 0.0)
        h = (h * h).astype(BF16)
        part = jnp.dot(h, w2_ref[c0:c0 + FF_CHUNK, :], preferred_element_type=F32)
        y = part if y is None else y + part
    return _layer_norm(alpha * x1 + y, g2, b2)


def _filter_kernel(z_ref, trow_ref, w1_ref, b1_ref, w2_ref, b2_ref, w3_ref, b3_ref, w4f_ref, w4b_ref,
                   freq_ref, delta_ref, bias_ref, o_ref, feat_ref, h_ref):
    hp = lax.Precision.HIGHEST

    @pl.when((pl.program_id(0) == 0) & (pl.program_id(1) == 0))
    def _():
        zt = z_ref[...]
        lane = lax.broadcasted_iota(jnp.int32, (1, zt.shape[1]), 1)
        feat_ref[...] = jnp.where(lane == 0, zt,
                                  jnp.where(lane <= FILTER_BANDS, jnp.cos(zt),
                                            jnp.where(lane <= 2 * FILTER_BANDS, -jnp.sin(zt), 0.0)))

    @pl.when(pl.program_id(1) == 0)
    def _():
        fr = freq_ref[0]
        h = jnp.sin(fr * (jnp.dot(feat_ref[...], w1_ref[0], precision=hp, preferred_element_type=F32) + b1_ref[0]))
        h = jnp.sin(fr * (jnp.dot(h, w2_ref[0], precision=hp, preferred_element_type=F32) + b2_ref[0]))
        h_ref[...] = jnp.sin(fr * (jnp.dot(h, w3_ref[0], precision=hp, preferred_element_type=F32) + b3_ref[0]))

    h = h_ref[...]
    nt = (((1,), (1,)), ((), ()))
    kf = lax.dot_general(w4f_ref[0], h, nt, precision=hp, preferred_element_type=F32)
    kb = lax.dot_general(w4b_ref[0], h, nt, precision=hp, preferred_element_type=F32)
    n_taps = kf.shape[1]
    q = lax.broadcasted_iota(jnp.int32, (1, n_taps), 1)
    decay = jnp.exp(-trow_ref[...] * delta_ref[...])
    k = jnp.where(q >= n_taps // 2, kf, kb) * decay
    k = jnp.where(q == 0, 0.0, k)
    k = k / jnp.sum(jnp.abs(k), axis=1, keepdims=True)
    o_ref[0] = jnp.where(q == n_taps // 2, k + bias_ref[0], k)


def _hyena_filters(seq_len, flt_w1, flt_b1, flt_w2, flt_b2, flt_w3, flt_b3, flt_w4, flt_freq, hy_bias):
    n_layers, _, order = flt_w1.shape
    half = flt_w4.shape[2] // 2
    n_taps = 2 * seq_len
    t = np.linspace(0.0, 1.0, seq_len, dtype=np.float32)
    w = np.float32(2.0 * math.pi / seq_len) * np.arange(seq_len, dtype=np.float32)
    bands = np.linspace(1e-4, FILTER_BANDS - 1, FILTER_BANDS, dtype=np.float32)
    pos = np.abs(np.arange(n_taps) - seq_len) % seq_len
    ang = (bands[None, :] * w[:, None])[pos]
    ztab = np.concatenate([t[pos][:, None], ang, ang,
                           np.zeros((n_taps, LANES - FILTER_EMB), np.float32)], axis=1)
    trow = t[pos][None, :]
    max_decay = math.log(DECAY_TARGET) / FAST_DECAY_PCT
    min_decay = math.log(DECAY_TARGET) / SLOW_DECAY_PCT
    deltas = np.abs(np.linspace(min_decay, max_decay, half, dtype=np.float32))[:, None]
    w1p = jnp.pad(flt_w1.astype(F32), ((0, 0), (0, LANES - FILTER_EMB), (0, 0)))
    w4t = jnp.swapaxes(flt_w4.astype(F32), 1, 2)
    row = lambda a: a.astype(F32)[:, None, :]
    cc = LANES
    n_cc = half // cc
    lay = lambda shape: pl.BlockSpec((1,) + shape, lambda j, c: (j, 0, 0))
    return pl.pallas_call(
        _filter_kernel,
        grid=(n_layers, n_cc),
        in_specs=[
            _const_spec((n_taps, LANES)), _const_spec((1, n_taps)),
            lay((LANES, order)), lay((1, order)), lay((order, order)), lay((1, order)),
            lay((order, order)), lay((1, order)),
            pl.BlockSpec((1, cc, order), lambda j, c: (j, c, 0)),
            pl.BlockSpec((1, cc, order), lambda j, c: (j, n_cc + c, 0)),
            lay((1, order)),
            pl.BlockSpec((cc, 1), lambda j, c: (c, 0)),
            pl.BlockSpec((1, cc, 1), lambda j, c: (j, c, 0)),
        ],
        out_specs=pl.BlockSpec((1, cc, n_taps), lambda j, c: (j, c, 0)),
        out_shape=jax.ShapeDtypeStruct((n_layers, half, n_taps), F32),
        scratch_shapes=[pltpu.VMEM((n_taps, LANES), F32), pltpu.VMEM((n_taps, order), F32)],
        compiler_params=pltpu.CompilerParams(dimension_semantics=("arbitrary", "arbitrary"),
                                             vmem_limit_bytes=VMEM_LIMIT),
        name="hyena_filter",
    )(ztab, trow, w1p, row(flt_b1), flt_w2.astype(F32), row(flt_b2), flt_w3.astype(F32), row(flt_b3),
      w4t, w4t, row(flt_freq), deltas, hy_bias.astype(F32)[:, :, None])


def _even_in_kernel(sizes, *refs):
    n_x = len(sizes)
    w_ref, p_ref, ya_ref, u_ref, x0_ref = refs[n_x:]
    xb = _routed_read(refs[:n_x], sizes, pl.program_id(0)).astype(BF16)
    nt = (((1,), (1,)), ((), ()))
    pt = lax.dot_general(w_ref[0], xb, nt, preferred_element_type=F32)
    cc = pt.shape[0] // 6
    seq = pt.shape[1]
    a_b, a_c, a_h, p0, p1, pv = [pt[i * cc:(i + 1) * cc] for i in range(6)]
    prm = p_ref[0]
    col = lambda j: prm[:, j:j + 1]
    lane = lax.broadcasted_iota(jnp.int32, (1, seq), 1)

    def conv3(z, j):
        zp = jnp.where(lane == 0, 0.0, pltpu.roll(z, 1, 1))
        zn = jnp.where(lane == seq - 1, 0.0, pltpu.roll(z, seq - 1, 1))
        return col(j) * zp + col(j + 1) * z + col(j + 2) * zn

    ya_ref[0] = a_b * conv3(a_c * a_h, 0)
    x0_ref[0] = conv3(p0, 3) + col(12)
    x1 = conv3(p1, 6) + col(13)
    v = conv3(pv, 9) + col(14)
    u_ref[0] = x1 * v


def _even_in(x_parts, w_chunks, prm):
    sizes = tuple(p.shape[0] for p in x_parts)
    n_b = sum(sizes)
    _, seq, d = x_parts[0].shape
    n_cc, rows, _ = w_chunks.shape
    cc = rows // 6
    half = n_cc * cc
    out = jax.ShapeDtypeStruct((n_b, half, seq), F32)
    out_spec = pl.BlockSpec((1, cc, seq), lambda b, c: (b, c, 0))
    return pl.pallas_call(
        functools.partial(_even_in_kernel, sizes),
        grid=(n_b, n_cc),
        in_specs=_routed_specs(x_parts, n_cc, (seq, d), lambda c: (0, 0)) + [
            pl.BlockSpec((1, rows, d), lambda b, c: (c, 0, 0)),
            pl.BlockSpec((1, cc, 16), lambda b, c: (c, 0, 0)),
        ],
        out_specs=[out_spec, out_spec, out_spec],
        out_shape=[out, out, out],
        compiler_params=pltpu.CompilerParams(dimension_semantics=("arbitrary", "arbitrary"),
                                             vmem_limit_bytes=VMEM_LIMIT),
        name="even_in",
    )(*x_parts, w_chunks, prm)


def _long_conv_kernel(kk_ref, u_ref, o_ref, toep_a_ref, toep_b_ref, us_ref, yc_ref):
    n_b, n_ch, seq = u_ref.shape
    blk = MXU_DIM
    n_blk = seq // blk

    uc = pltpu.einshape("bcl->cbl", u_ref[...]).astype(BF16)
    for j in range(n_blk):
        us_ref[:, pl.ds(j * n_b, n_b), :] = uc[:, :, j * blk:(j + 1) * blk]

    def build_toeplitz(c, dst_ref):
        taps = kk_ref[pl.ds(c, 1), :]
        toep = pltpu.roll(jnp.broadcast_to(taps, (blk, 2 * seq)), 0, 1, stride=1, stride_axis=0)
        dst_ref[...] = toep.astype(BF16)

    def convolve(c, src_ref):
        acc = [None] * n_blk
        for dd in range(-(n_blk - 1), n_blk):
            j0, j1 = max(0, -dd), min(n_blk, n_blk - dd)
            t_dd = src_ref[:, pl.ds(seq + blk * dd, blk)]
            r = jnp.dot(us_ref[c, pl.ds(j0 * n_b, (j1 - j0) * n_b), :], t_dd, preferred_element_type=F32)
            for n, j in enumerate(range(j0, j1)):
                part = r[n * n_b:(n + 1) * n_b]
                acc[j + dd] = part if acc[j + dd] is None else acc[j + dd] + part
        for i in range(n_blk):
            yc_ref[c, :, pl.ds(i * blk, blk)] = acc[i]

    build_toeplitz(0, toep_a_ref)

    def channel_pair(p, carry):
        build_toeplitz(2 * p + 1, toep_b_ref)
        convolve(2 * p, toep_a_ref)
        build_toeplitz(jnp.minimum(2 * p + 2, n_ch - 1), toep_a_ref)
        convolve(2 * p + 1, toep_b_ref)
        return carry

    lax.fori_loop(0, n_ch // 2, channel_pair, 0)
    o_ref[...] = pltpu.einshape("cbl->bcl", yc_ref[...])


def _long_conv(kk, u_t):
    n_b, half, seq = u_t.shape
    cc = E2_CHANNELS
    assert cc % 2 == 0
    blk_spec = pl.BlockSpec((n_b, cc, seq), lambda c: (0, c, 0))
    return pl.pallas_call(
        _long_conv_kernel,
        grid=(half // cc,),
        in_specs=[pl.BlockSpec((cc, 2 * seq), lambda c: (c, 0)), blk_spec],
        out_specs=blk_spec,
        out_shape=jax.ShapeDtypeStruct((n_b, half, seq), F32),
        scratch_shapes=[pltpu.VMEM((MXU_DIM, 2 * seq), BF16), pltpu.VMEM((MXU_DIM, 2 * seq), BF16),
                        pltpu.VMEM((cc, (seq // MXU_DIM) * n_b, MXU_DIM), BF16),
                        pltpu.VMEM((cc, n_b, seq), F32)],
        compiler_params=pltpu.CompilerParams(dimension_semantics=("arbitrary",),
                                             vmem_limit_bytes=VMEM_LIMIT),
        name="long_conv",
    )(kk, u_t)


def _even_out_kernel(alpha, sizes, *refs):
    n_x = len(sizes)
    ya_ref, cv_ref, x0_ref, wo_ref, g1_ref, b1_ref, w1_ref, w2_ref, g2_ref, b2_ref, o_ref = refs[n_x:]
    x = _routed_read(refs[:n_x], sizes, pl.program_id(0))
    ycat = jnp.concatenate([ya_ref[0], x0_ref[0] * cv_ref[0]], axis=0).astype(BF16)
    m = lax.dot_general(ycat, wo_ref[...], (((0,), (0,)), ((), ())), preferred_element_type=F32)
    x1 = _layer_norm(alpha * x + m, g1_ref[...], b1_ref[...])
    o_ref[0] = _mlp_tail(x1, alpha, w1_ref, w2_ref, g2_ref[...], b2_ref[...])


def _even_out(alpha, x_parts, ya_t, cv_t, x0_t, wo, g1, b1, w1, w2, g2, b2):
    sizes = tuple(p.shape[0] for p in x_parts)
    n_b = sum(sizes)
    _, seq, d = x_parts[0].shape
    half = ya_t.shape[1]
    tm = TOKEN_TILE
    tok = pl.BlockSpec((1, tm, d), lambda b, t: (b, t, 0))
    chan = pl.BlockSpec((1, half, tm), lambda b, t: (b, 0, t))
    consts = [wo, g1, b1, w1, w2, g2, b2]
    return pl.pallas_call(
        functools.partial(_even_out_kernel, alpha, sizes),
        grid=(n_b, seq // tm),
        in_specs=(_routed_specs(x_parts, seq // tm, (tm, d), lambda t: (t, 0)) + [chan, chan, chan]
                  + [_const_spec(a.shape) for a in consts]),
        out_specs=tok,
        out_shape=jax.ShapeDtypeStruct((n_b, seq, d), F32),
        compiler_params=pltpu.CompilerParams(dimension_semantics=("arbitrary", "arbitrary"),
                                             vmem_limit_bytes=VMEM_LIMIT),
        name="even_out",
    )(*x_parts, ya_t, cv_t, x0_t, *consts)


def _conv_in_rows(tm, halo):
    off = halo - CONF_K // 2
    return tm + SUBLANES * ((off + CONF_K - 1) // SUBLANES)


def _odd_kernel(alpha, seq, out_sizes, x_ref, xp_ref, xn_ref, wp1_ref, bp1_ref, dw_ref, dwb_ref, lg_ref, lb_ref,
                wp2_ref, bp2_ref, g1_ref, b1_ref, w1_ref, w2_ref, g2_ref, b2_ref, *rest):
    out_refs = rest[:len(out_sizes)]
    glu_ref, gs_ref, cv_ref = rest[len(out_sizes):]
    tm, d = x_ref.shape[1], x_ref.shape[2]
    halo = xp_ref.shape[1]
    n_lt = d // LANES
    b, t = pl.program_id(0), pl.program_id(1)

    xcat = jnp.concatenate([xp_ref[0], x_ref[0], xn_ref[0]], axis=0).astype(BF16)
    h = jnp.dot(xcat, wp1_ref[...], preferred_element_type=F32) + bp1_ref[...]
    glu = h[:, :d] * jax.nn.sigmoid(h[:, d:])
    for j in range(n_lt):
        glu_ref[j] = glu[:, j * LANES:(j + 1) * LANES]

    @pl.when(t == 0)
    def _():
        glu_ref[:, pl.ds(0, halo), :] = jnp.zeros((n_lt, halo, LANES), F32)

    @pl.when(t == seq // tm - 1)
    def _():
        glu_ref[:, pl.ds(tm + halo, halo), :] = jnp.zeros((n_lt, halo, LANES), F32)

    off = halo - CONF_K // 2
    n_in = _conv_in_rows(tm, halo)

    def conv_lane_tile(i, carry):
        for r in range(1, SUBLANES):
            gs_ref[r - 1] = glu_ref[i, pl.ds(r, n_in), :]
        for r0 in range(0, tm, CONV_ROWS):
            acc = None
            for k in range(CONF_K):
                a8, r = SUBLANES * ((off + k) // SUBLANES), (off + k) % SUBLANES
                if r == 0:
                    g = glu_ref[i, pl.ds(r0 + a8, CONV_ROWS), :]
                else:
                    g = gs_ref[r - 1, pl.ds(r0 + a8, CONV_ROWS), :]
                term = dw_ref[i, pl.ds(k, 1), :] * g
                acc = term if acc is None else acc + term
            cv_ref[i, pl.ds(r0, CONV_ROWS), :] = acc
        return carry

    lax.fori_loop(0, n_lt, conv_lane_tile, 0)

    cv = jnp.concatenate([cv_ref[j] for j in range(n_lt)], axis=1)
    c = _layer_norm(cv + dwb_ref[...], lg_ref[...], lb_ref[...])
    c = c * jax.nn.sigmoid(c)
    m = jnp.dot(c.astype(BF16), wp2_ref[...], preferred_element_type=F32) + bp2_ref[...]
    x1 = _layer_norm(alpha * x_ref[0] + m, g1_ref[...], b1_ref[...])
    y = _mlp_tail(x1, alpha, w1_ref, w2_ref, g2_ref[...], b2_ref[...])
    if len(out_refs) == 1:
        out_refs[0][0] = y
    else:
        start = 0
        for o_ref, n_rows in zip(out_refs, out_sizes):
            @pl.when((b >= start) & (b < start + n_rows))
            def _(o_ref=o_ref):
                o_ref[0] = y
            start += n_rows


def _odd_layer(alpha, x, wp1, bp1, dw, dwb, lg, lb, wp2, bp2, g1, b1, w1, w2, g2, b2, out_sizes=None):
    n_b, seq, d = x.shape
    out_sizes = (n_b,) if out_sizes is None else tuple(out_sizes)
    assert sum(out_sizes) == n_b
    tm, halo = TOKEN_TILE, CONF_HALO
    per = tm // halo
    last = seq // halo - 1
    n_t = seq // tm
    tok = pl.BlockSpec((1, tm, d), lambda b, t: (b, t, 0))
    prev = pl.BlockSpec((1, halo, d), lambda b, t: (b, jnp.maximum(t * per - 1, 0), 0))
    nxt = pl.BlockSpec((1, halo, d), lambda b, t: (b, jnp.minimum((t + 1) * per, last), 0))

    def out_spec(start, n_rows):
        def index(b, t):
            pos = jnp.clip((b - start) * n_t + t, 0, n_rows * n_t - 1)
            return (pos // n_t, pos % n_t, 0)
        return pl.BlockSpec((1, tm, d), index)

    starts = [sum(out_sizes[:i]) for i in range(len(out_sizes))]
    n_lt = d // LANES
    dw_t = dw.reshape(dw.shape[0], n_lt, LANES).transpose(1, 0, 2)
    consts = [wp1, bp1, dw_t, dwb, lg, lb, wp2, bp2, g1, b1, w1, w2, g2, b2]
    return pl.pallas_call(
        functools.partial(_odd_kernel, alpha, seq, out_sizes),
        grid=(n_b, n_t),
        in_specs=[tok, prev, nxt] + [_const_spec(a.shape) for a in consts],
        out_specs=[out_spec(st, nr) for st, nr in zip(starts, out_sizes)],
        out_shape=[jax.ShapeDtypeStruct((nr, seq, d), F32) for nr in out_sizes],
        scratch_shapes=[pltpu.VMEM((n_lt, tm + 2 * halo, LANES), F32),
                        pltpu.VMEM((SUBLANES - 1, _conv_in_rows(tm, halo), LANES), F32),
                        pltpu.VMEM((n_lt, tm, LANES), F32)],
        compiler_params=pltpu.CompilerParams(dimension_semantics=("arbitrary", "arbitrary"),
                                             vmem_limit_bytes=VMEM_LIMIT),
        name="odd_layer",
    )(x, x, x, *consts)


def kernel(x_prompt, x_sample, e_w_in, e_conv_a, e_short_w, e_short_b, e_flt_w1, e_flt_b1, e_flt_w2, e_flt_b2,
           e_flt_w3, e_flt_b3, e_flt_w4, e_flt_freq, e_hy_bias, e_w_out, o_w_pw1, o_b_pw1, o_dw_w, o_dw_b,
           o_ln_g, o_ln_b, o_w_pw2, o_b_pw2, ln1_g, ln1_b, mlp_w1, mlp_w2, ln2_g, ln2_b):
    depth = ln1_g.shape[0]
    alpha = (2 * depth) ** 0.25
    n_prompt = x_prompt.shape[0]
    seq, d = x_prompt.shape[1], x_prompt.shape[2]
    half = d // 2
    assert x_sample.shape[1:] == (seq, d)
    assert seq % TOKEN_TILE == 0 and seq % MXU_DIM == 0 and half % E1_CHUNK == 0 and half % E2_CHANNELS == 0
    assert CONF_HALO >= CONF_K // 2 and o_dw_w.shape[1] == CONF_K and e_conv_a.shape[1] == SHORT_K

    x_parts = (x_prompt, x_sample)
    row = lambda a: a.astype(F32)[None, :]

    kk_all = _hyena_filters(seq, e_flt_w1, e_flt_b1, e_flt_w2, e_flt_b2, e_flt_w3, e_flt_b3, e_flt_w4,
                            e_flt_freq, e_hy_bias)

    n_cc = half // E1_CHUNK
    for i in range(depth):
        j = i // 2
        w1 = mlp_w1[i].astype(BF16)
        w2 = mlp_w2[i].astype(BF16)
        g1, b1, g2, b2 = row(ln1_g[i]), row(ln1_b[i]), row(ln2_g[i]), row(ln2_b[i])
        if i % 2 == 0:
            w_t = e_w_in[j].T.reshape(6, n_cc, E1_CHUNK, d).transpose(1, 0, 2, 3).reshape(n_cc, 6 * E1_CHUNK, d)
            sw, sb = e_short_w[j], e_short_b[j]
            cols = [e_conv_a[j][k] for k in range(SHORT_K)]
            for g in range(3):
                cols += [sw[k, g * half:(g + 1) * half] for k in range(SHORT_K)]
            cols += [sb[g * half:(g + 1) * half] for g in range(3)]
            cols += [jnp.zeros((half,), F32)]
            prm = jnp.stack(cols, axis=1).astype(F32).reshape(n_cc, E1_CHUNK, 16)
            ya_t, u_t, x0_t = _even_in(x_parts, w_t.astype(BF16), prm)
            cv_t = _long_conv(kk_all[j], u_t)
            x_parts = (_even_out(alpha, x_parts, ya_t, cv_t, x0_t, e_w_out[j].astype(BF16),
                                 g1, b1, w1, w2, g2, b2),)
        else:
            dw = jnp.pad(o_dw_w[j].astype(F32), ((0, 32 - CONF_K), (0, 0)))
            (x,) = x_parts
            out_sizes = (n_prompt, x.shape[0] - n_prompt) if i == depth - 1 else None
            x_parts = tuple(_odd_layer(alpha, x, o_w_pw1[j].astype(BF16), row(o_b_pw1[j]), dw, row(o_dw_b[j]),
                                       row(o_ln_g[j]), row(o_ln_b[j]), o_w_pw2[j].astype(BF16), row(o_b_pw2[j]),
                                       g1, b1, w1, w2, g2, b2, out_sizes))
    if len(x_parts) == 1:
        return (x_parts[0][:n_prompt], x_parts[0][n_prompt:])
    return x_parts
```

```python
import functools
import math

import jax
import jax.numpy as jnp
import numpy as np
from jax import lax
from jax.experimental import pallas as pl
from jax.experimental.pallas import tpu as pltpu

F32 = jnp.float32
BF16 = jnp.bfloat16

SHORT_K = 3
CONF_K = 31
FILTER_BANDS = 16
FILTER_EMB = 1 + 2 * FILTER_BANDS
DECAY_TARGET = 1e-2
FAST_DECAY_PCT = 0.3
SLOW_DECAY_PCT = 1.5
LN_EPS = 1e-5

LANES = 128
SUBLANES = 8
MXU_DIM = 256

TOKEN_TILE = 512
CONF_HALO = 16
CONV_ROWS = 128
FF_CHUNK = 2048
E1_CHUNK = 128
E2_CHANNELS = 8
VMEM_LIMIT = 58 * 1024 * 1024


def _const_spec(shape):
    nd = len(shape)
    return pl.BlockSpec(shape, lambda *_: (0,) * nd, pipeline_mode=pl.Buffered(1))


def _layer_spec(stacked, layer):
    tail = stacked.shape[1:]
    return pl.BlockSpec((None,) + tail, lambda *_: (layer,) + (0,) * len(tail), pipeline_mode=pl.Buffered(1))


def _param_specs(params):
    operands = [p[0] if isinstance(p, tuple) else p for p in params]
    specs = [_layer_spec(*p) if isinstance(p, tuple) else _const_spec(p.shape) for p in params]
    return operands, specs


def _routed_specs(parts, n_k, tail_block, tail_index):
    specs, start = [], 0
    for p in parts:
        n = p.shape[0]

        def index(b, k, start=start, n=n):
            pos = jnp.clip((b - start) * n_k + k, 0, n * n_k - 1)
            return (pos // n_k,) + tail_index(pos % n_k)

        specs.append(pl.BlockSpec((1,) + tail_block, index))
        start += n
    return specs


def _routed_read(refs, sizes, b):
    x, start = refs[0][0], sizes[0]
    for r, n in zip(refs[1:], sizes[1:]):
        x = jnp.where(b >= start, r[0], x)
        start += n
    return x


def _layer_norm(x, g, b):
    mu = jnp.mean(x, axis=-1, keepdims=True)
    xc = x - mu
    var = jnp.mean(xc * xc, axis=-1, keepdims=True)
    return xc * lax.rsqrt(var + LN_EPS) * g + b


def _mlp_tail(x1, alpha, w1_ref, w2_ref, g2, b2):
    xb = x1.astype(BF16)
    d_ff = w1_ref.shape[1]
    y = None
    for c0 in range(0, d_ff, FF_CHUNK):
        h = jnp.dot(xb, w1_ref[:, c0:c0 + FF_CHUNK], preferred_element_type=F32)
        h = jnp.maximum(h, 0.0)
        h = (h * h).astype(BF16)
        part = jnp.dot(h, w2_ref[c0:c0 + FF_CHUNK, :], preferred_element_type=F32)
        y = part if y is None else y + part
    return _layer_norm(alpha * x1 + y, g2, b2)


def _filter_kernel(z_ref, trow_ref, w1_ref, b1_ref, w2_ref, b2_ref, w3_ref, b3_ref, w4f_ref, w4b_ref,
                   freq_ref, delta_ref, bias_ref, o_ref, feat_ref, h_ref):
    hp = lax.Precision.HIGHEST

    @pl.when((pl.program_id(0) == 0) & (pl.program_id(1) == 0))
    def _():
        zt = z_ref[...]
        lane = lax.broadcasted_iota(jnp.int32, (1, zt.shape[1]), 1)
        feat_ref[...] = jnp.where(lane == 0, zt,
                                  jnp.where(lane <= FILTER_BANDS, jnp.cos(zt),
                                            jnp.where(lane <= 2 * FILTER_BANDS, -jnp.sin(zt), 0.0)))

    @pl.when(pl.program_id(1) == 0)
    def _():
        fr = freq_ref[0]
        h = jnp.sin(fr * (jnp.dot(feat_ref[...], w1_ref[0], precision=hp, preferred_element_type=F32) + b1_ref[0]))
        h = jnp.sin(fr * (jnp.dot(h, w2_ref[0], precision=hp, preferred_element_type=F32) + b2_ref[0]))
        h_ref[...] = jnp.sin(fr * (jnp.dot(h, w3_ref[0], precision=hp, preferred_element_type=F32) + b3_ref[0]))

    h = h_ref[...]
    nt = (((1,), (1,)), ((), ()))
    kf = lax.dot_general(w4f_ref[0], h, nt, precision=hp, preferred_element_type=F32)
    kb = lax.dot_general(w4b_ref[0], h, nt, precision=hp, preferred_element_type=F32)
    n_taps = kf.shape[1]
    q = lax.broadcasted_iota(jnp.int32, (1, n_taps), 1)
    decay = jnp.exp(-trow_ref[...] * delta_ref[...])
    k = jnp.where(q >= n_taps // 2, kf, kb) * decay
    k = jnp.where(q == 0, 0.0, k)
    k = k / jnp.sum(jnp.abs(k), axis=1, keepdims=True)
    o_ref[0] = jnp.where(q == n_taps // 2, k + bias_ref[0], k)


def _hyena_filters(seq_len, flt_w1, flt_b1, flt_w2, flt_b2, flt_w3, flt_b3, flt_w4, flt_freq, hy_bias):
    n_layers, _, order = flt_w1.shape
    half = flt_w4.shape[2] // 2
    n_taps = 2 * seq_len
    t = np.linspace(0.0, 1.0, seq_len, dtype=np.float32)
    w = np.float32(2.0 * math.pi / seq_len) * np.arange(seq_len, dtype=np.float32)
    bands = np.linspace(1e-4, FILTER_BANDS - 1, FILTER_BANDS, dtype=np.float32)
    pos = np.abs(np.arange(n_taps) - seq_len) % seq_len
    ang = (bands[None, :] * w[:, None])[pos]
    ztab = np.concatenate([t[pos][:, None], ang, ang,
                           np.zeros((n_taps, LANES - FILTER_EMB), np.float32)], axis=1)
    trow = t[pos][None, :]
    max_decay = math.log(DECAY_TARGET) / FAST_DECAY_PCT
    min_decay = math.log(DECAY_TARGET) / SLOW_DECAY_PCT
    deltas = np.abs(np.linspace(min_decay, max_decay, half, dtype=np.float32))[:, None]
    w1p = jnp.pad(flt_w1.astype(F32), ((0, 0), (0, LANES - FILTER_EMB), (0, 0)))
    w4t = jnp.swapaxes(flt_w4.astype(F32), 1, 2)
    row = lambda a: a.astype(F32)[:, None, :]
    cc = LANES
    n_cc = half // cc
    lay = lambda shape: pl.BlockSpec((1,) + shape, lambda j, c: (j, 0, 0))
    return pl.pallas_call(
        _filter_kernel,
        grid=(n_layers, n_cc),
        in_specs=[
            _const_spec((n_taps, LANES)), _const_spec((1, n_taps)),
            lay((LANES, order)), lay((1, order)), lay((order, order)), lay((1, order)),
            lay((order, order)), lay((1, order)),
            pl.BlockSpec((1, cc, order), lambda j, c: (j, c, 0)),
            pl.BlockSpec((1, cc, order), lambda j, c: (j, n_cc + c, 0)),
            lay((1, order)),
            pl.BlockSpec((cc, 1), lambda j, c: (c, 0)),
            pl.BlockSpec((1, cc, 1), lambda j, c: (j, c, 0)),
        ],
        out_specs=pl.BlockSpec((1, cc, n_taps), lambda j, c: (j, c, 0)),
        out_shape=jax.ShapeDtypeStruct((n_layers, half, n_taps), F32),
        scratch_shapes=[pltpu.VMEM((n_taps, LANES), F32), pltpu.VMEM((n_taps, order), F32)],
        compiler_params=pltpu.CompilerParams(dimension_semantics=("arbitrary", "arbitrary"),
                                             vmem_limit_bytes=VMEM_LIMIT),
        name="hyena_filter",
    )(ztab, trow, w1p, row(flt_b1), flt_w2.astype(F32), row(flt_b2), flt_w3.astype(F32), row(flt_b3),
      w4t, w4t, row(flt_freq), deltas, hy_bias.astype(F32)[:, :, None])


def _even_in_kernel(sizes, *refs):
    n_x = len(sizes)
    w_ref, p_ref, ya_ref, u_ref, x0_ref = refs[n_x:]
    xb = _routed_read(refs[:n_x], sizes, pl.program_id(0)).astype(BF16)
    nt = (((1,), (1,)), ((), ()))
    pt = lax.dot_general(w_ref[0], xb, nt, preferred_element_type=F32)
    cc = pt.shape[0] // 6
    seq = pt.shape[1]
    a_b, a_c, a_h, p0, p1, pv = [pt[i * cc:(i + 1) * cc] for i in range(6)]
    prm = p_ref[0]
    col = lambda j: prm[:, j:j + 1]
    lane = lax.broadcasted_iota(jnp.int32, (1, seq), 1)

    def conv3(z, j):
        zp = jnp.where(lane == 0, 0.0, pltpu.roll(z, 1, 1))
        zn = jnp.where(lane == seq - 1, 0.0, pltpu.roll(z, seq - 1, 1))
        return col(j) * zp + col(j + 1) * z + col(j + 2) * zn

    ya_ref[0] = a_b * conv3(a_c * a_h, 0)
    x0_ref[0] = conv3(p0, 3) + col(12)
    x1 = conv3(p1, 6) + col(13)
    v = conv3(pv, 9) + col(14)
    u_ref[0] = x1 * v


def _even_in(x_parts, w_chunks, prm):
    sizes = tuple(p.shape[0] for p in x_parts)
    n_b = sum(sizes)
    _, seq, d = x_parts[0].shape
    n_cc, rows, _ = w_chunks.shape
    cc = rows // 6
    half = n_cc * cc
    out = jax.ShapeDtypeStruct((n_b, half, seq), F32)
    out_spec = pl.BlockSpec((1, cc, seq), lambda b, c: (b, c, 0))
    return pl.pallas_call(
        functools.partial(_even_in_kernel, sizes),
        grid=(n_b, n_cc),
        in_specs=_routed_specs(x_parts, n_cc, (seq, d), lambda c: (0, 0)) + [
            pl.BlockSpec((1, rows, d), lambda b, c: (c, 0, 0)),
            pl.BlockSpec((1, cc, 16), lambda b, c: (c, 0, 0)),
        ],
        out_specs=[out_spec, out_spec, out_spec],
        out_shape=[out, out, out],
        compiler_params=pltpu.CompilerParams(dimension_semantics=("arbitrary", "arbitrary"),
                                             vmem_limit_bytes=VMEM_LIMIT),
        name="even_in",
    )(*x_parts, w_chunks, prm)


def _long_conv_kernel(kk_ref, u_ref, o_ref, toep_a_ref, toep_b_ref, us_ref, yc_ref):
    n_b, n_ch, seq = u_ref.shape
    blk = MXU_DIM
    n_blk = seq // blk

    uc = pltpu.einshape("bcl->cbl", u_ref[...].astype(BF16))
    for j in range(n_blk):
        us_ref[:, pl.ds(j * n_b, n_b), :] = uc[:, :, j * blk:(j + 1) * blk]

    def build_toeplitz(c, dst_ref):
        taps = kk_ref[pl.ds(c, 1), :]
        toep = pltpu.roll(jnp.broadcast_to(taps, (blk // 2, 2 * seq)), 0, 1, stride=1, stride_axis=0)
        dst_ref[...] = toep.astype(BF16)

    def convolve(c, src_ref):
        acc = [None] * n_blk
        for dd in range(-(n_blk - 1), n_blk):
            j0, j1 = max(0, -dd), min(n_blk, n_blk - dd)
            s0 = seq + blk * dd
            t_dd = jnp.concatenate([src_ref[:, pl.ds(s0, blk)], src_ref[:, pl.ds(s0 - blk // 2, blk)]], axis=0)
            r = jnp.dot(us_ref[c, pl.ds(j0 * n_b, (j1 - j0) * n_b), :], t_dd, preferred_element_type=F32)
            for n, j in enumerate(range(j0, j1)):
                part = r[n * n_b:(n + 1) * n_b]
                acc[j + dd] = part if acc[j + dd] is None else acc[j + dd] + part
        for i in range(n_blk):
            yc_ref[c, :, pl.ds(i * blk, blk)] = acc[i]

    build_toeplitz(0, toep_a_ref)

    def channel_pair(p, carry):
        build_toeplitz(2 * p + 1, toep_b_ref)
        convolve(2 * p, toep_a_ref)
        build_toeplitz(jnp.minimum(2 * p + 2, n_ch - 1), toep_a_ref)
        convolve(2 * p + 1, toep_b_ref)
        return carry

    lax.fori_loop(0, n_ch // 2, channel_pair, 0)
    o_ref[...] = pltpu.einshape("cbl->bcl", yc_ref[...])


def _long_conv(kk, u_t):
    n_b, half, seq = u_t.shape
    cc = E2_CHANNELS
    assert cc % 2 == 0
    blk_spec = pl.BlockSpec((n_b, cc, seq), lambda c: (0, c, 0))
    return pl.pallas_call(
        _long_conv_kernel,
        grid=(half // cc,),
        in_specs=[pl.BlockSpec((cc, 2 * seq), lambda c: (c, 0)), blk_spec],
        out_specs=blk_spec,
        out_shape=jax.ShapeDtypeStruct((n_b, half, seq), F32),
        scratch_shapes=[pltpu.VMEM((MXU_DIM // 2, 2 * seq), BF16), pltpu.VMEM((MXU_DIM // 2, 2 * seq), BF16),
                        pltpu.VMEM((cc, (seq // MXU_DIM) * n_b, MXU_DIM), BF16),
                        pltpu.VMEM((cc, n_b, seq), F32)],
        compiler_params=pltpu.CompilerParams(dimension_semantics=("arbitrary",),
                                             vmem_limit_bytes=VMEM_LIMIT),
        name="long_conv",
    )(kk, u_t)


def _even_out_kernel(alpha, sizes, *refs):
    n_x = len(sizes)
    ya_ref, cv_ref, x0_ref, wo_ref, g1_ref, b1_ref, w1_ref, w2_ref, g2_ref, b2_ref, o_ref = refs[n_x:]
    x = _routed_read(refs[:n_x], sizes, pl.program_id(0))
    ycat = jnp.concatenate([ya_ref[0], x0_ref[0] * cv_ref[0]], axis=0).astype(BF16)
    m = lax.dot_general(ycat, wo_ref[...], (((0,), (0,)), ((), ())), preferred_element_type=F32)
    x1 = _layer_norm(alpha * x + m, g1_ref[...], b1_ref[...])
    o_ref[0] = _mlp_tail(x1, alpha, w1_ref, w2_ref, g2_ref[...], b2_ref[...])


def _even_out(alpha, x_parts, ya_t, cv_t, x0_t, wo, g1, b1, w1, w2, g2, b2):
    sizes = tuple(p.shape[0] for p in x_parts)
    n_b = sum(sizes)
    _, seq, d = x_parts[0].shape
    half = ya_t.shape[1]
    tm = TOKEN_TILE
    tok = pl.BlockSpec((1, tm, d), lambda b, t: (b, t, 0))
    chan = pl.BlockSpec((1, half, tm), lambda b, t: (b, 0, t))
    consts, const_specs = _param_specs([wo, g1, b1, w1, w2, g2, b2])
    return pl.pallas_call(
        functools.partial(_even_out_kernel, alpha, sizes),
        grid=(n_b, seq // tm),
        in_specs=_routed_specs(x_parts, seq // tm, (tm, d), lambda t: (t, 0)) + [chan, chan, chan] + const_specs,
        out_specs=tok,
        out_shape=jax.ShapeDtypeStruct((n_b, seq, d), F32),
        compiler_params=pltpu.CompilerParams(dimension_semantics=("arbitrary", "arbitrary"),
                                             vmem_limit_bytes=VMEM_LIMIT),
        name="even_out",
    )(*x_parts, ya_t, cv_t, x0_t, *consts)


def _conv_in_rows(tm, halo):
    off = halo - CONF_K // 2
    return tm + SUBLANES * ((off + CONF_K - 1) // SUBLANES)


def _odd_kernel(alpha, seq, out_sizes, x_ref, xp_ref, xn_ref, wp1_ref, bp1_ref, dw_ref, dwb_ref, lg_ref, lb_ref,
                wp2_ref, bp2_ref, g1_ref, b1_ref, w1_ref, w2_ref, g2_ref, b2_ref, *rest):
    out_refs = rest[:len(out_sizes)]
    glu_ref, gs_ref, cv_ref = rest[len(out_sizes):]
    tm, d = x_ref.shape[1], x_ref.shape[2]
    halo = xp_ref.shape[1]
    n_lt = d // LANES
    b, t = pl.program_id(0), pl.program_id(1)

    xcat = jnp.concatenate([xp_ref[0], x_ref[0], xn_ref[0]], axis=0).astype(BF16)
    h = jnp.dot(xcat, wp1_ref[...], preferred_element_type=F32) + bp1_ref[...]
    glu = h[:, :d] * jax.nn.sigmoid(h[:, d:])
    for j in range(n_lt):
        glu_ref[j] = glu[:, j * LANES:(j + 1) * LANES]

    @pl.when(t == 0)
    def _():
        glu_ref[:, pl.ds(0, halo), :] = jnp.zeros((n_lt, halo, LANES), F32)

    @pl.when(t == seq // tm - 1)
    def _():
        glu_ref[:, pl.ds(tm + halo, halo), :] = jnp.zeros((n_lt, halo, LANES), F32)

    off = halo - CONF_K // 2
    n_in = _conv_in_rows(tm, halo)

    def conv_lane_tile(i, carry):
        for r in range(1, SUBLANES):
            gs_ref[r - 1] = glu_ref[i, pl.ds(r, n_in), :]
        for r0 in range(0, tm, CONV_ROWS):
            acc = None
            for k in range(CONF_K):
                a8, r = SUBLANES * ((off + k) // SUBLANES), (off + k) % SUBLANES
                if r == 0:
                    g = glu_ref[i, pl.ds(r0 + a8, CONV_ROWS), :]
                else:
                    g = gs_ref[r - 1, pl.ds(r0 + a8, CONV_ROWS), :]
                term = dw_ref[i, pl.ds(k, 1), :] * g
                acc = term if acc is None else acc + term
            cv_ref[i, pl.ds(r0, CONV_ROWS), :] = acc
        return carry

    lax.fori_loop(0, n_lt, conv_lane_tile, 0)

    cv = jnp.concatenate([cv_ref[j] for j in range(n_lt)], axis=1)
    c = _layer_norm(cv + dwb_ref[...], lg_ref[...], lb_ref[...])
    c = c * jax.nn.sigmoid(c)
    m = jnp.dot(c.astype(BF16), wp2_ref[...], preferred_element_type=F32) + bp2_ref[...]
    x1 = _layer_norm(alpha * x_ref[0] + m, g1_ref[...], b1_ref[...])
    y = _mlp_tail(x1, alpha, w1_ref, w2_ref, g2_ref[...], b2_ref[...])
    if len(out_refs) == 1:
        out_refs[0][0] = y
    else:
        start = 0
        for o_ref, n_rows in zip(out_refs, out_sizes):
            @pl.when((b >= start) & (b < start + n_rows))
            def _(o_ref=o_ref):
                o_ref[0] = y
            start += n_rows


def _odd_layer(alpha, x, wp1, bp1, dw, dwb, lg, lb, wp2, bp2, g1, b1, w1, w2, g2, b2, out_sizes=None):
    n_b, seq, d = x.shape
    out_sizes = (n_b,) if out_sizes is None else tuple(out_sizes)
    assert sum(out_sizes) == n_b
    tm, halo = TOKEN_TILE, CONF_HALO
    per = tm // halo
    last = seq // halo - 1
    n_t = seq // tm
    tok = pl.BlockSpec((1, tm, d), lambda b, t: (b, t, 0))
    prev = pl.BlockSpec((1, halo, d), lambda b, t: (b, jnp.maximum(t * per - 1, 0), 0))
    nxt = pl.BlockSpec((1, halo, d), lambda b, t: (b, jnp.minimum((t + 1) * per, last), 0))

    def out_spec(start, n_rows):
        def index(b, t):
            pos = jnp.clip((b - start) * n_t + t, 0, n_rows * n_t - 1)
            return (pos // n_t, pos % n_t, 0)
        return pl.BlockSpec((1, tm, d), index)

    starts = [sum(out_sizes[:i]) for i in range(len(out_sizes))]
    n_lt = d // LANES
    dw_t = dw.reshape(dw.shape[0], n_lt, LANES).transpose(1, 0, 2)
    consts, const_specs = _param_specs([wp1, bp1, dw_t, dwb, lg, lb, wp2, bp2, g1, b1, w1, w2, g2, b2])
    return pl.pallas_call(
        functools.partial(_odd_kernel, alpha, seq, out_sizes),
        grid=(n_b, n_t),
        in_specs=[tok, prev, nxt] + const_specs,
        out_specs=[out_spec(st, nr) for st, nr in zip(starts, out_sizes)],
        out_shape=[jax.ShapeDtypeStruct((nr, seq, d), F32) for nr in out_sizes],
        scratch_shapes=[pltpu.VMEM((n_lt, tm + 2 * halo, LANES), F32),
                        pltpu.VMEM((SUBLANES - 1, _conv_in_rows(tm, halo), LANES), F32),
                        pltpu.VMEM((n_lt, tm, LANES), F32)],
        compiler_params=pltpu.CompilerParams(dimension_semantics=("arbitrary", "arbitrary"),
                                             vmem_limit_bytes=VMEM_LIMIT),
        name="odd_layer",
    )(x, x, x, *consts)


def kernel(x_prompt, x_sample, e_w_in, e_conv_a, e_short_w, e_short_b, e_flt_w1, e_flt_b1, e_flt_w2, e_flt_b2,
           e_flt_w3, e_flt_b3, e_flt_w4, e_flt_freq, e_hy_bias, e_w_out, o_w_pw1, o_b_pw1, o_dw_w, o_dw_b,
           o_ln_g, o_ln_b, o_w_pw2, o_b_pw2, ln1_g, ln1_b, mlp_w1, mlp_w2, ln2_g, ln2_b):
    depth = ln1_g.shape[0]
    alpha = (2 * depth) ** 0.25
    n_prompt = x_prompt.shape[0]
    seq, d = x_prompt.shape[1], x_prompt.shape[2]
    half = d // 2
    assert x_sample.shape[1:] == (seq, d)
    assert seq % TOKEN_TILE == 0 and seq % MXU_DIM == 0 and half % E1_CHUNK == 0 and half % E2_CHANNELS == 0
    assert CONF_HALO >= CONF_K // 2 and o_dw_w.shape[1] == CONF_K and e_conv_a.shape[1] == SHORT_K

    x_parts = (x_prompt, x_sample)
    row = lambda a: a.astype(F32)[None, :]

    kk_all = _hyena_filters(seq, e_flt_w1, e_flt_b1, e_flt_w2, e_flt_b2, e_flt_w3, e_flt_b3, e_flt_w4,
                            e_flt_freq, e_hy_bias)

    mlp_w1_b, mlp_w2_b = mlp_w1.astype(BF16), mlp_w2.astype(BF16)
    e_w_out_b, o_w_pw1_b, o_w_pw2_b = e_w_out.astype(BF16), o_w_pw1.astype(BF16), o_w_pw2.astype(BF16)

    n_cc = half // E1_CHUNK
    for i in range(depth):
        j = i // 2
        w1 = (mlp_w1_b, i)
        w2 = (mlp_w2_b, i)
        g1, b1, g2, b2 = row(ln1_g[i]), row(ln1_b[i]), row(ln2_g[i]), row(ln2_b[i])
        if i % 2 == 0:
            w_t = e_w_in[j].T.reshape(6, n_cc, E1_CHUNK, d).transpose(1, 0, 2, 3).reshape(n_cc, 6 * E1_CHUNK, d)
            sw, sb = e_short_w[j], e_short_b[j]
            cols = [e_conv_a[j][k] for k in range(SHORT_K)]
            for g in range(3):
                cols += [sw[k, g * half:(g + 1) * half] for k in range(SHORT_K)]
            cols += [sb[g * half:(g + 1) * half] for g in range(3)]
            cols += [jnp.zeros((half,), F32)]
            prm = jnp.stack(cols, axis=1).astype(F32).reshape(n_cc, E1_CHUNK, 16)
            ya_t, u_t, x0_t = _even_in(x_parts, w_t.astype(BF16), prm)
            cv_t = _long_conv(kk_all[j], u_t)
            x_parts = (_even_out(alpha, x_parts, ya_t, cv_t, x0_t, (e_w_out_b, j), g1, b1, w1, w2, g2, b2),)
        else:
            dw = jnp.pad(o_dw_w[j].astype(F32), ((0, 32 - CONF_K), (0, 0)))
            (x,) = x_parts
            out_sizes = (n_prompt, x.shape[0] - n_prompt) if i == depth - 1 else None
            x_parts = tuple(_odd_layer(alpha, x, (o_w_pw1_b, j), row(o_b_pw1[j]), dw, row(o_dw_b[j]),
                                       row(o_ln_g[j]), row(o_ln_b[j]), (o_w_pw2_b, j), row(o_b_pw2[j]),
                                       g1, b1, w1, w2, g2, b2, out_sizes))
    if len(x_parts) == 1:
        return (x_parts[0][:n_prompt], x_parts[0][n_prompt:])
    return x_parts
```

```python
import functools
import math

import jax
import jax.numpy as jnp
import numpy as np
from jax import lax
from jax.experimental import pallas as pl
from jax.experimental.pallas import tpu as pltpu

F32 = jnp.float32
BF16 = jnp.bfloat16

SHORT_K = 3
CONF_K = 31
FILTER_BANDS = 16
FILTER_EMB = 1 + 2 * FILTER_BANDS
DECAY_TARGET = 1e-2
FAST_DECAY_PCT = 0.3
SLOW_DECAY_PCT = 1.5
LN_EPS = 1e-5

LANES = 128
SUBLANES = 8
MXU_DIM = 256

TOKEN_TILE = 512
CONF_HALO = 16
CONV_ROWS = 128
FF_CHUNK = 2048
E1_CHUNK = 128
E2_CHANNELS = 8
VMEM_LIMIT = 58 * 1024 * 1024


def _const_spec(shape):
    nd = len(shape)
    return pl.BlockSpec(shape, lambda *_: (0,) * nd, pipeline_mode=pl.Buffered(1))


def _layer_spec(stacked, layer):
    tail = stacked.shape[1:]
    return pl.BlockSpec((None,) + tail, lambda *_: (layer,) + (0,) * len(tail), pipeline_mode=pl.Buffered(1))


def _param_specs(params):
    operands = [p[0] if isinstance(p, tuple) else p for p in params]
    specs = [_layer_spec(*p) if isinstance(p, tuple) else _const_spec(p.shape) for p in params]
    return operands, specs


def _routed_specs(parts, n_k, tail_block, tail_index):
    specs, start = [], 0
    for p in parts:
        n = p.shape[0]

        def index(b, k, start=start, n=n):
            pos = jnp.clip((b - start) * n_k + k, 0, n * n_k - 1)
            return (pos // n_k,) + tail_index(pos % n_k)

        specs.append(pl.BlockSpec((1,) + tail_block, index))
        start += n
    return specs


def _routed_read(refs, sizes, b):
    x, start = refs[0][0], sizes[0]
    for r, n in zip(refs[1:], sizes[1:]):
        x = jnp.where(b >= start, r[0], x)
        start += n
    return x


def _layer_norm(x, g, b):
    mu = jnp.mean(x, axis=-1, keepdims=True)
    xc = x - mu
    var = jnp.mean(xc * xc, axis=-1, keepdims=True)
    return xc * lax.rsqrt(var + LN_EPS) * g + b


def _mlp_tail(x1, alpha, w1_ref, w2_ref, g2, b2):
    xb = x1.astype(BF16)
    d_ff = w1_ref.shape[1]
    y = None
    for c0 in range(0, d_ff, FF_CHUNK):
        h = jnp.dot(xb, w1_ref[:, c0:c0 + FF_CHUNK], preferred_element_type=F32)
        h = jnp.maximum(h, 0.0)
        h = (h * h).astype(BF16)
        part = jnp.dot(h, w2_ref[c0:c0 + FF_CHUNK, :], preferred_element_type=F32)
        y = part if y is None else y + part
    return _layer_norm(alpha * x1 + y, g2, b2)


def _filter_kernel(z_ref, trow_ref, w1_ref, b1_ref, w2_ref, b2_ref, w3_ref, b3_ref, w4f_ref, w4b_ref,
                   freq_ref, delta_ref, bias_ref, o_ref, feat_ref, h_ref):
    hp = lax.Precision.HIGHEST

    @pl.when((pl.program_id(0) == 0) & (pl.program_id(1) == 0))
    def _():
        zt = z_ref[...]
        lane = lax.broadcasted_iota(jnp.int32, (1, zt.shape[1]), 1)
        feat_ref[...] = jnp.where(lane == 0, zt,
                                  jnp.where(lane <= FILTER_BANDS, jnp.cos(zt),
                                            jnp.where(lane <= 2 * FILTER_BANDS, -jnp.sin(zt), 0.0)))

    @pl.when(pl.program_id(1) == 0)
    def _():
        fr = freq_ref[0]
        h = jnp.sin(fr * (jnp.dot(feat_ref[...], w1_ref[0], precision=hp, preferred_element_type=F32) + b1_ref[0]))
        h = jnp.sin(fr * (jnp.dot(h, w2_ref[0], precision=hp, preferred_element_type=F32) + b2_ref[0]))
        h_ref[...] = jnp.sin(fr * (jnp.dot(h, w3_ref[0], precision=hp, preferred_element_type=F32) + b3_ref[0]))

    h = h_ref[...]
    nt = (((1,), (1,)), ((), ()))
    kf = lax.dot_general(w4f_ref[0], h, nt, precision=hp, preferred_element_type=F32)
    kb = lax.dot_general(w4b_ref[0], h, nt, precision=hp, preferred_element_type=F32)
    n_taps = kf.shape[1]
    q = lax.broadcasted_iota(jnp.int32, (1, n_taps), 1)
    decay = jnp.exp(-trow_ref[...] * delta_ref[...])
    k = jnp.where(q >= n_taps // 2, kf, kb) * decay
    k = jnp.where(q == 0, 0.0, k)
    k = k / jnp.sum(jnp.abs(k), axis=1, keepdims=True)
    o_ref[0] = jnp.where(q == n_taps // 2, k + bias_ref[0], k)


def _hyena_filters(seq_len, flt_w1, flt_b1, flt_w2, flt_b2, flt_w3, flt_b3, flt_w4, flt_freq, hy_bias):
    n_layers, _, order = flt_w1.shape
    half = flt_w4.shape[2] // 2
    n_taps = 2 * seq_len
    t = np.linspace(0.0, 1.0, seq_len, dtype=np.float32)
    w = np.float32(2.0 * math.pi / seq_len) * np.arange(seq_len, dtype=np.float32)
    bands = np.linspace(1e-4, FILTER_BANDS - 1, FILTER_BANDS, dtype=np.float32)
    pos = np.abs(np.arange(n_taps) - seq_len) % seq_len
    ang = (bands[None, :] * w[:, None])[pos]
    ztab = np.concatenate([t[pos][:, None], ang, ang,
                           np.zeros((n_taps, LANES - FILTER_EMB), np.float32)], axis=1)
    trow = t[pos][None, :]
    max_decay = math.log(DECAY_TARGET) / FAST_DECAY_PCT
    min_decay = math.log(DECAY_TARGET) / SLOW_DECAY_PCT
    deltas = np.abs(np.linspace(min_decay, max_decay, half, dtype=np.float32))[:, None]
    w1p = jnp.pad(flt_w1.astype(F32), ((0, 0), (0, LANES - FILTER_EMB), (0, 0)))
    w4t = jnp.swapaxes(flt_w4.astype(F32), 1, 2)
    row = lambda a: a.astype(F32)[:, None, :]
    cc = LANES
    n_cc = half // cc
    lay = lambda shape: pl.BlockSpec((1,) + shape, lambda j, c: (j, 0, 0))
    return pl.pallas_call(
        _filter_kernel,
        grid=(n_layers, n_cc),
        in_specs=[
            _const_spec((n_taps, LANES)), _const_spec((1, n_taps)),
            lay((LANES, order)), lay((1, order)), lay((order, order)), lay((1, order)),
            lay((order, order)), lay((1, order)),
            pl.BlockSpec((1, cc, order), lambda j, c: (j, c, 0)),
            pl.BlockSpec((1, cc, order), lambda j, c: (j, n_cc + c, 0)),
            lay((1, order)),
            pl.BlockSpec((cc, 1), lambda j, c: (c, 0)),
            pl.BlockSpec((1, cc, 1), lambda j, c: (j, c, 0)),
        ],
        out_specs=pl.BlockSpec((1, cc, n_taps), lambda j, c: (j, c, 0)),
        out_shape=jax.ShapeDtypeStruct((n_layers, half, n_taps), F32),
        scratch_shapes=[pltpu.VMEM((n_taps, LANES), F32), pltpu.VMEM((n_taps, order), F32)],
        compiler_params=pltpu.CompilerParams(dimension_semantics=("arbitrary", "arbitrary"),
                                             vmem_limit_bytes=VMEM_LIMIT),
        name="hyena_filter",
    )(ztab, trow, w1p, row(flt_b1), flt_w2.astype(F32), row(flt_b2), flt_w3.astype(F32), row(flt_b3),
      w4t, w4t, row(flt_freq), deltas, hy_bias.astype(F32)[:, :, None])


def _even_in_kernel(sizes, *refs):
    n_x = len(sizes)
    w_ref, p_ref, ya_ref, u_ref, x0_ref = refs[n_x:]
    xb = _routed_read(refs[:n_x], sizes, pl.program_id(0)).astype(BF16)
    nt = (((1,), (1,)), ((), ()))
    pt = lax.dot_general(w_ref[0], xb, nt, preferred_element_type=F32)
    cc = pt.shape[0] // 6
    seq = pt.shape[1]
    a_b, a_c, a_h, p0, p1, pv = [pt[i * cc:(i + 1) * cc] for i in range(6)]
    prm = p_ref[0]
    col = lambda j: prm[:, j:j + 1]
    lane = lax.broadcasted_iota(jnp.int32, (1, seq), 1)

    def conv3(z, j):
        zp = jnp.where(lane == 0, 0.0, pltpu.roll(z, 1, 1))
        zn = jnp.where(lane == seq - 1, 0.0, pltpu.roll(z, seq - 1, 1))
        return col(j) * zp + col(j + 1) * z + col(j + 2) * zn

    ya_ref[0] = a_b * conv3(a_c * a_h, 0)
    x0_ref[0] = conv3(p0, 3) + col(12)
    x1 = conv3(p1, 6) + col(13)
    v = conv3(pv, 9) + col(14)
    u_ref[0] = x1 * v


def _even_in(x_parts, w_chunks, prm):
    sizes = tuple(p.shape[0] for p in x_parts)
    n_b = sum(sizes)
    _, seq, d = x_parts[0].shape
    n_cc, rows, _ = w_chunks.shape
    cc = rows // 6
    half = n_cc * cc
    out = jax.ShapeDtypeStruct((n_b, half, seq), F32)
    out_spec = pl.BlockSpec((1, cc, seq), lambda b, c: (b, c, 0))
    return pl.pallas_call(
        functools.partial(_even_in_kernel, sizes),
        grid=(n_b, n_cc),
        in_specs=_routed_specs(x_parts, n_cc, (seq, d), lambda c: (0, 0)) + [
            pl.BlockSpec((1, rows, d), lambda b, c: (c, 0, 0)),
            pl.BlockSpec((1, cc, 16), lambda b, c: (c, 0, 0)),
        ],
        out_specs=[out_spec, out_spec, out_spec],
        out_shape=[out, out, out],
        compiler_params=pltpu.CompilerParams(dimension_semantics=("arbitrary", "arbitrary"),
                                             vmem_limit_bytes=VMEM_LIMIT),
        name="even_in",
    )(*x_parts, w_chunks, prm)


def _long_conv_kernel(kk_ref, u_ref, o_ref, toep_a_ref, toep_b_ref, us_ref, yc_ref):
    n_b, n_ch, seq = u_ref.shape
    blk = MXU_DIM
    n_blk = seq // blk

    uc = pltpu.einshape("bcl->cbl", u_ref[...].astype(BF16))
    for j in range(n_blk):
        us_ref[:, pl.ds(j * n_b, n_b), :] = uc[:, :, j * blk:(j + 1) * blk]

    def build_toeplitz(c, dst_ref):
        taps = kk_ref[pl.ds(c, 1), :]
        toep = pltpu.roll(jnp.broadcast_to(taps, (blk // 2, 2 * seq)), 0, 1, stride=1, stride_axis=0)
        dst_ref[...] = toep.astype(BF16)

    def convolve(c, src_ref):
        acc = [None] * n_blk
        for dd in range(-(n_blk - 1), n_blk):
            j0, j1 = max(0, -dd), min(n_blk, n_blk - dd)
            s0 = seq + blk * dd
            t_dd = jnp.concatenate([src_ref[:, pl.ds(s0, blk)], src_ref[:, pl.ds(s0 - blk // 2, blk)]], axis=0)
            r = jnp.dot(us_ref[c, pl.ds(j0 * n_b, (j1 - j0) * n_b), :], t_dd, preferred_element_type=F32)
            for n, j in enumerate(range(j0, j1)):
                part = r[n * n_b:(n + 1) * n_b]
                acc[j + dd] = part if acc[j + dd] is None else acc[j + dd] + part
        for i in range(n_blk):
            yc_ref[c, :, pl.ds(i * blk, blk)] = acc[i]

    build_toeplitz(0, toep_a_ref)

    def channel_pair(p, carry):
        build_toeplitz(2 * p + 1, toep_b_ref)
        convolve(2 * p, toep_a_ref)
        build_toeplitz(jnp.minimum(2 * p + 2, n_ch - 1), toep_a_ref)
        convolve(2 * p + 1, toep_b_ref)
        return carry

    lax.fori_loop(0, n_ch // 2, channel_pair, 0)
    o_ref[...] = pltpu.einshape("cbl->bcl", yc_ref[...])


def _long_conv(kk, u_t):
    n_b, half, seq = u_t.shape
    cc = E2_CHANNELS
    assert cc % 2 == 0
    blk_spec = pl.BlockSpec((n_b, cc, seq), lambda c: (0, c, 0))
    return pl.pallas_call(
        _long_conv_kernel,
        grid=(half // cc,),
        in_specs=[pl.BlockSpec((cc, 2 * seq), lambda c: (c, 0)), blk_spec],
        out_specs=blk_spec,
        out_shape=jax.ShapeDtypeStruct((n_b, half, seq), F32),
        scratch_shapes=[pltpu.VMEM((MXU_DIM // 2, 2 * seq), BF16), pltpu.VMEM((MXU_DIM // 2, 2 * seq), BF16),
                        pltpu.VMEM((cc, (seq // MXU_DIM) * n_b, MXU_DIM), BF16),
                        pltpu.VMEM((cc, n_b, seq), F32)],
        compiler_params=pltpu.CompilerParams(dimension_semantics=("arbitrary",),
                                             vmem_limit_bytes=VMEM_LIMIT),
        name="long_conv",
    )(kk, u_t)


def _even_out_kernel(alpha, sizes, *refs):
    n_x = len(sizes)
    ya_ref, cv_ref, x0_ref, wo_ref, g1_ref, b1_ref, w1_ref, w2_ref, g2_ref, b2_ref, o_ref = refs[n_x:]
    x = _routed_read(refs[:n_x], sizes, pl.program_id(0))
    ycat = jnp.concatenate([ya_ref[0], x0_ref[0] * cv_ref[0]], axis=0).astype(BF16)
    m = lax.dot_general(ycat, wo_ref[...], (((0,), (0,)), ((), ())), preferred_element_type=F32)
    x1 = _layer_norm(alpha * x + m, g1_ref[...], b1_ref[...])
    o_ref[0] = _mlp_tail(x1, alpha, w1_ref, w2_ref, g2_ref[...], b2_ref[...])


def _even_out(alpha, x_parts, ya_t, cv_t, x0_t, wo, g1, b1, w1, w2, g2, b2):
    sizes = tuple(p.shape[0] for p in x_parts)
    n_b = sum(sizes)
    _, seq, d = x_parts[0].shape
    half = ya_t.shape[1]
    tm = TOKEN_TILE
    tok = pl.BlockSpec((1, tm, d), lambda b, t: (b, t, 0))
    chan = pl.BlockSpec((1, half, tm), lambda b, t: (b, 0, t))
    consts, const_specs = _param_specs([wo, g1, b1, w1, w2, g2, b2])
    return pl.pallas_call(
        functools.partial(_even_out_kernel, alpha, sizes),
        grid=(n_b, seq // tm),
        in_specs=_routed_specs(x_parts, seq // tm, (tm, d), lambda t: (t, 0)) + [chan, chan, chan] + const_specs,
        out_specs=tok,
        out_shape=jax.ShapeDtypeStruct((n_b, seq, d), F32),
        compiler_params=pltpu.CompilerParams(dimension_semantics=("arbitrary", "arbitrary"),
                                             vmem_limit_bytes=VMEM_LIMIT),
        name="even_out",
    )(*x_parts, ya_t, cv_t, x0_t, *consts)


def _odd_kernel(alpha, seq, out_sizes, x_ref, xp_ref, xn_ref, wp1_ref, bp1_ref, dw_ref, dwb_ref, lg_ref, lb_ref,
                wp2_ref, bp2_ref, g1_ref, b1_ref, w1_ref, w2_ref, g2_ref, b2_ref, *rest):
    out_refs = rest[:len(out_sizes)]
    glu_ref, cv_ref = rest[len(out_sizes):]
    tm, d = x_ref.shape[1], x_ref.shape[2]
    halo = xp_ref.shape[1]
    n_lt = d // LANES
    b, t = pl.program_id(0), pl.program_id(1)

    xcat = jnp.concatenate([xp_ref[0], x_ref[0], xn_ref[0]], axis=0).astype(BF16)
    h = jnp.dot(xcat, wp1_ref[...], preferred_element_type=F32) + bp1_ref[...]
    glu = h[:, :d] * jax.nn.sigmoid(h[:, d:])
    for j in range(n_lt):
        glu_ref[j] = glu[:, j * LANES:(j + 1) * LANES]

    @pl.when(t == 0)
    def _():
        glu_ref[:, pl.ds(0, halo), :] = jnp.zeros((n_lt, halo, LANES), F32)

    @pl.when(t == seq // tm - 1)
    def _():
        glu_ref[:, pl.ds(tm + halo, halo), :] = jnp.zeros((n_lt, halo, LANES), F32)

    off = halo - CONF_K // 2

    def conv_lane_tile(i, carry):
        for r0 in range(0, tm, CONV_ROWS):
            acc = None
            for k in range(CONF_K):
                term = dw_ref[i, pl.ds(k, 1), :] * glu_ref[i, pl.ds(r0 + off + k, CONV_ROWS), :]
                acc = term if acc is None else acc + term
            cv_ref[i, pl.ds(r0, CONV_ROWS), :] = acc
        return carry

    lax.fori_loop(0, n_lt, conv_lane_tile, 0)

    cv = jnp.concatenate([cv_ref[j] for j in range(n_lt)], axis=1)
    c = _layer_norm(cv + dwb_ref[...], lg_ref[...], lb_ref[...])
    c = c * jax.nn.sigmoid(c)
    m = jnp.dot(c.astype(BF16), wp2_ref[...], preferred_element_type=F32) + bp2_ref[...]
    x1 = _layer_norm(alpha * x_ref[0] + m, g1_ref[...], b1_ref[...])
    y = _mlp_tail(x1, alpha, w1_ref, w2_ref, g2_ref[...], b2_ref[...])
    if len(out_refs) == 1:
        out_refs[0][0] = y
    else:
        start = 0
        for o_ref, n_rows in zip(out_refs, out_sizes):
            @pl.when((b >= start) & (b < start + n_rows))
            def _(o_ref=o_ref):
                o_ref[0] = y
            start += n_rows


def _odd_layer(alpha, x, wp1, bp1, dw, dwb, lg, lb, wp2, bp2, g1, b1, w1, w2, g2, b2, out_sizes=None):
    n_b, seq, d = x.shape
    out_sizes = (n_b,) if out_sizes is None else tuple(out_sizes)
    assert sum(out_sizes) == n_b
    tm, halo = TOKEN_TILE, CONF_HALO
    per = tm // halo
    last = seq // halo - 1
    n_t = seq // tm
    tok = pl.BlockSpec((1, tm, d), lambda b, t: (b, t, 0))
    prev = pl.BlockSpec((1, halo, d), lambda b, t: (b, jnp.maximum(t * per - 1, 0), 0))
    nxt = pl.BlockSpec((1, halo, d), lambda b, t: (b, jnp.minimum((t + 1) * per, last), 0))

    def out_spec(start, n_rows):
        def index(b, t):
            pos = jnp.clip((b - start) * n_t + t, 0, n_rows * n_t - 1)
            return (pos // n_t, pos % n_t, 0)
        return pl.BlockSpec((1, tm, d), index)

    starts = [sum(out_sizes[:i]) for i in range(len(out_sizes))]
    n_lt = d // LANES
    dw_t = dw.reshape(dw.shape[0], n_lt, LANES).transpose(1, 0, 2)
    consts, const_specs = _param_specs([wp1, bp1, dw_t, dwb, lg, lb, wp2, bp2, g1, b1, w1, w2, g2, b2])
    return pl.pallas_call(
        functools.partial(_odd_kernel, alpha, seq, out_sizes),
        grid=(n_b, n_t),
        in_specs=[tok, prev, nxt] + const_specs,
        out_specs=[out_spec(st, nr) for st, nr in zip(starts, out_sizes)],
        out_shape=[jax.ShapeDtypeStruct((nr, seq, d), F32) for nr in out_sizes],
        scratch_shapes=[pltpu.VMEM((n_lt, tm + 2 * halo, LANES), F32), pltpu.VMEM((n_lt, tm, LANES), F32)],
        compiler_params=pltpu.CompilerParams(dimension_semantics=("arbitrary", "arbitrary"),
                                             vmem_limit_bytes=VMEM_LIMIT),
        name="odd_layer",
    )(x, x, x, *consts)


def kernel(x_prompt, x_sample, e_w_in, e_conv_a, e_short_w, e_short_b, e_flt_w1, e_flt_b1, e_flt_w2, e_flt_b2,
           e_flt_w3, e_flt_b3, e_flt_w4, e_flt_freq, e_hy_bias, e_w_out, o_w_pw1, o_b_pw1, o_dw_w, o_dw_b,
           o_ln_g, o_ln_b, o_w_pw2, o_b_pw2, ln1_g, ln1_b, mlp_w1, mlp_w2, ln2_g, ln2_b):
    depth = ln1_g.shape[0]
    alpha = (2 * depth) ** 0.25
    n_prompt = x_prompt.shape[0]
    seq, d = x_prompt.shape[1], x_prompt.shape[2]
    half = d // 2
    assert x_sample.shape[1:] == (seq, d)
    assert seq % TOKEN_TILE == 0 and seq % MXU_DIM == 0 and half % E1_CHUNK == 0 and half % E2_CHANNELS == 0
    assert CONF_HALO >= CONF_K // 2 and o_dw_w.shape[1] == CONF_K and e_conv_a.shape[1] == SHORT_K

    x_parts = (x_prompt, x_sample)
    row = lambda a: a.astype(F32)[None, :]

    kk_all = _hyena_filters(seq, e_flt_w1, e_flt_b1, e_flt_w2, e_flt_b2, e_flt_w3, e_flt_b3, e_flt_w4,
                            e_flt_freq, e_hy_bias)

    mlp_w1_b, mlp_w2_b = mlp_w1.astype(BF16), mlp_w2.astype(BF16)
    e_w_out_b, o_w_pw1_b, o_w_pw2_b = e_w_out.astype(BF16), o_w_pw1.astype(BF16), o_w_pw2.astype(BF16)

    n_cc = half // E1_CHUNK
    for i in range(depth):
        j = i // 2
        w1 = (mlp_w1_b, i)
        w2 = (mlp_w2_b, i)
        g1, b1, g2, b2 = row(ln1_g[i]), row(ln1_b[i]), row(ln2_g[i]), row(ln2_b[i])
        if i % 2 == 0:
            w_t = e_w_in[j].T.reshape(6, n_cc, E1_CHUNK, d).transpose(1, 0, 2, 3).reshape(n_cc, 6 * E1_CHUNK, d)
            sw, sb = e_short_w[j], e_short_b[j]
            cols = [e_conv_a[j][k] for k in range(SHORT_K)]
            for g in range(3):
                cols += [sw[k, g * half:(g + 1) * half] for k in range(SHORT_K)]
            cols += [sb[g * half:(g + 1) * half] for g in range(3)]
            cols += [jnp.zeros((half,), F32)]
            prm = jnp.stack(cols, axis=1).astype(F32).reshape(n_cc, E1_CHUNK, 16)
            ya_t, u_t, x0_t = _even_in(x_parts, w_t.astype(BF16), prm)
            cv_t = _long_conv(kk_all[j], u_t)
            x_parts = (_even_out(alpha, x_parts, ya_t, cv_t, x0_t, (e_w_out_b, j), g1, b1, w1, w2, g2, b2),)
        else:
            dw = jnp.pad(o_dw_w[j].astype(F32), ((0, 32 - CONF_K), (0, 0)))
            (x,) = x_parts
            out_sizes = (n_prompt, x.shape[0] - n_prompt) if i == depth - 1 else None
            x_parts = tuple(_odd_layer(alpha, x, (o_w_pw1_b, j), row(o_b_pw1[j]), dw, row(o_dw_b[j]),
                                       row(o_ln_g[j]), row(o_ln_b[j]), (o_w_pw2_b, j), row(o_b_pw2[j]),
                                       g1, b1, w1, w2, g2, b2, out_sizes))
    if len(x_parts) == 1:
        return (x_parts[0][:n_prompt], x_parts[0][n_prompt:])
    return x_parts
```
